```python
import math
import jax
import jax.numpy as jnp
from jax import lax
import numpy as np

D_MODEL = 1024
BATCH = 16
SEQ = 2048
DEPTH = 1
DEC_BATCH = 128
DEC_SEQ = 8
PAST_LEN = 8192
PAGE_SIZE = 128

H_A = 8
DH = 64
A_WIDTH = H_A * DH
S_WIDTH = D_MODEL // 2
GROUP_CH = 16
N_GROUPS = S_WIDTH // GROUP_CH
STATE_P = 64
D_FF = 4 * D_MODEL
Q_BLOCK = 128
N_IN = 3 * A_WIDTH + H_A + S_WIDTH + 2 * D_MODEL
IN_SPLITS = [A_WIDTH, 2 * A_WIDTH, 3 * A_WIDTH, 3 * A_WIDTH + H_A,
             3 * A_WIDTH + H_A + S_WIDTH, 3 * A_WIDTH + H_A + S_WIDTH + D_MODEL]
ATTN_SCALE = 1.0 / math.sqrt(DH)
NORM_EPS = 1e-6
POOL_NUM = 5
POOL_DEN = 4

kernel_name = "fox_s5_gated_hybrid_step"


def _rmsnorm(x, g):
    xf = x.astype(jnp.float32)
    y = xf * lax.rsqrt(jnp.mean(xf * xf, axis=-1, keepdims=True) + NORM_EPS)
    return (y * g.astype(jnp.float32)).astype(x.dtype)


def _adaln(c, w_ada, b_ada):
    mod = jax.nn.silu(c) @ w_ada + b_ada
    return jnp.split(mod[:, None, :], 6, axis=-1)


def _fox_attend(q, k, v, gq, gk, q_pos, k_pos):
    s = jnp.einsum('bthd,bshd->bhts', q, k, preferred_element_type=jnp.float32) * ATTN_SCALE
    decay = gq.astype(jnp.float32).transpose(0, 2, 1)[:, :, :, None] - gk.astype(jnp.float32).transpose(0, 2, 1)[:, :, None, :]
    mask = k_pos[None, :] <= q_pos[:, None]
    s = jnp.where(mask[None, None], s + decay, jnp.finfo(jnp.float32).min)
    p = jax.nn.softmax(s, axis=-1)
    return jnp.einsum('bhts,bshd->bthd', p.astype(v.dtype), v)


def _s5(u, h0_re, h0_im, lam_re, lam_im, log_dt, b_re, b_im, c_re, c_im, d):
    bsz, L, _ = u.shape
    f32 = jnp.float32
    ug = u.astype(f32).reshape(bsz, L, N_GROUPS, GROUP_CH)
    dt = jnp.exp(log_dt.astype(f32))[:, None]
    lr, li = lam_re.astype(f32), lam_im.astype(f32)
    mag = jnp.exp(lr * dt)
    ab_re, ab_im = mag * jnp.cos(li * dt), mag * jnp.sin(li * dt)
    nr, ni = ab_re - 1.0, ab_im
    den = lr * lr + li * li
    f_re = (nr * lr + ni * li) / den
    f_im = (ni * lr - nr * li) / den
    br, bi = b_re.astype(f32), b_im.astype(f32)
    bb_re = f_re[..., None] * br - f_im[..., None] * bi
    bb_im = f_re[..., None] * bi + f_im[..., None] * br
    bu_re = jnp.einsum('blgc,gpc->blgp', ug, bb_re)
    bu_im = jnp.einsum('blgc,gpc->blgp', ug, bb_im)
    h0r, h0i = h0_re.astype(f32), h0_im.astype(f32)
    bu_re = bu_re.at[:, 0].add(ab_re * h0r - ab_im * h0i)
    bu_im = bu_im.at[:, 0].add(ab_re * h0i + ab_im * h0r)
    a_re = jnp.broadcast_to(ab_re, bu_re.shape)
    a_im = jnp.broadcast_to(ab_im, bu_im.shape)

    def combine(e1, e2):
        a1r, a1i, b1r, b1i = e1
        a2r, a2i, b2r, b2i = e2
        return (a1r * a2r - a1i * a2i, a1r * a2i + a1i * a2r,
                a2r * b1r - a2i * b1i + b2r, a2r * b1i + a2i * b1r + b2i)

    _, _, hr, hi = lax.associative_scan(combine, (a_re, a_im, bu_re, bu_im), axis=1)
    y = (jnp.einsum('blgp,gcp->blgc', hr, c_re.astype(f32))
         - jnp.einsum('blgp,gcp->blgc', hi, c_im.astype(f32)))
    y = y.reshape(bsz, L, S_WIDTH) + d.astype(f32) * u.astype(f32)
    return y, hr[:, -1], hi[:, -1]


def setup_inputs(seed: int = 0) -> dict:
    key = jax.random.key(seed)
    ks = jax.random.split(key, 40)
    f32 = jnp.float32
    n_pages = PAST_LEN // PAGE_SIZE
    n_used = DEC_BATCH * n_pages
    n_pool = (n_used * POOL_NUM) // POOL_DEN
    nrm = lambda k, shape, s: jax.random.normal(k, shape, f32) * s
    page_table = jax.random.permutation(ks[0], n_pool)[:n_used].reshape(DEC_BATCH, n_pages).astype(jnp.int32)
    lam_re = -0.5 + nrm(ks[1], (N_GROUPS, STATE_P), 0.01)
    lam_im = jnp.pi * jnp.arange(STATE_P, dtype=f32)[None, :] + nrm(ks[2], (N_GROUPS, STATE_P), 0.01)
    log_dt = jax.random.uniform(ks[3], (N_GROUPS,), f32, math.log(0.001), math.log(0.1))
    return {
        "x_prompt": nrm(ks[4], (BATCH, SEQ, D_MODEL), 1.0),
        "x_sample": nrm(ks[5], (DEC_BATCH, DEC_SEQ, D_MODEL), 1.0),
        "cache_k": nrm(ks[6], (n_pool, PAGE_SIZE, H_A, DH), 1.0),
        "cache_v": nrm(ks[7], (n_pool, PAGE_SIZE, H_A, DH), 1.0),
        "cache_logf": jax.nn.log_sigmoid(3.0 + nrm(ks[8], (n_pool, PAGE_SIZE, H_A), 0.5)),
        "state_ssm_re": nrm(ks[9], (DEC_BATCH, N_GROUPS, STATE_P), 0.5),
        "state_ssm_im": nrm(ks[10], (DEC_BATCH, N_GROUPS, STATE_P), 0.5),
        "page_table": page_table,
        "c_prompt": nrm(ks[11], (BATCH, D_MODEL), 1.0),
        "c_sample": nrm(ks[12], (DEC_BATCH, D_MODEL), 1.0),
        "w_ada": nrm(ks[13], (D_MODEL, 6 * D_MODEL), 0.5 * D_MODEL ** -0.5),
        "b_ada": nrm(ks[14], (6 * D_MODEL,), 0.02),
        "norm_mix_g": 1.0 + nrm(ks[15], (D_MODEL,), 0.02),
        "norm_ffn_g": 1.0 + nrm(ks[16], (D_MODEL,), 0.02),
        "w_in": nrm(ks[17], (D_MODEL, N_IN), D_MODEL ** -0.5),
        "b_fgate": 3.0 + nrm(ks[18], (H_A,), 0.1),
        "q_norm_g": 1.0 + nrm(ks[19], (DH,), 0.02),
        "k_norm_g": 1.0 + nrm(ks[20], (DH,), 0.02),
        "ssm_lambda_re": lam_re,
        "ssm_lambda_im": lam_im,
        "ssm_log_dt": log_dt,
        "ssm_b_re": nrm(ks[21], (N_GROUPS, STATE_P, GROUP_CH), (2 * GROUP_CH) ** -0.5),
        "ssm_b_im": nrm(ks[22], (N_GROUPS, STATE_P, GROUP_CH), (2 * GROUP_CH) ** -0.5),
        "ssm_c_re": nrm(ks[23], (N_GROUPS, GROUP_CH, STATE_P), (2 * STATE_P) ** -0.5),
        "ssm_c_im": nrm(ks[24], (N_GROUPS, GROUP_CH, STATE_P), (2 * STATE_P) ** -0.5),
        "ssm_d": nrm(ks[25], (S_WIDTH,), 0.5),
        "w_glu": nrm(ks[26], (S_WIDTH, S_WIDTH), S_WIDTH ** -0.5),
        "b_glu": nrm(ks[27], (S_WIDTH,), 0.02),
        "w_up_a": nrm(ks[28], (A_WIDTH, D_MODEL), A_WIDTH ** -0.5),
        "w_up_s": nrm(ks[29], (S_WIDTH, D_MODEL), S_WIDTH ** -0.5),
        "w_out": nrm(ks[30], (D_MODEL, D_MODEL), D_MODEL ** -0.5),
        "w_ffn_up": nrm(ks[31], (D_MODEL, D_FF), D_MODEL ** -0.5),
        "w_ffn_down": nrm(ks[32], (D_FF, D_MODEL), D_FF ** -0.5),
    }


def reference(x_prompt, x_sample, cache_k, cache_v, cache_logf, state_ssm_re, state_ssm_im, page_table,
              c_prompt, c_sample, w_ada, b_ada, norm_mix_g, norm_ffn_g, w_in, b_fgate, q_norm_g, k_norm_g,
              ssm_lambda_re, ssm_lambda_im, ssm_log_dt, ssm_b_re, ssm_b_im, ssm_c_re, ssm_c_im, ssm_d,
              w_glu, b_glu, w_up_a, w_up_s, w_out, w_ffn_up, w_ffn_down):
    f32 = jnp.float32

    def layer(x, c, attend, h0_re, h0_im):
        bsz, T, _ = x.shape
        sh1, sc1, g1, sh2, sc2, g2 = _adaln(c, w_ada, b_ada)
        h = _rmsnorm(x, norm_mix_g) * (1.0 + sc1) + sh1
        z = h @ w_in
        q, k, v, f_logit, u, ga, gs = jnp.split(z, IN_SPLITS, axis=-1)
        q = _rmsnorm(q.reshape(bsz, T, H_A, DH), q_norm_g)
        k = _rmsnorm(k.reshape(bsz, T, H_A, DH), k_norm_g)
        v = v.reshape(bsz, T, H_A, DH)
        lf = jax.nn.log_sigmoid((f_logit + b_fgate).astype(f32))
        a = attend(q, k, v, lf).reshape(bsz, T, A_WIDTH).astype(x.dtype)
        y_s, hT_re, hT_im = _s5(u, h0_re, h0_im, ssm_lambda_re, ssm_lambda_im, ssm_log_dt,
                                ssm_b_re, ssm_b_im, ssm_c_re, ssm_c_im, ssm_d)
        y_s = jax.nn.gelu(y_s).astype(x.dtype)
        s = y_s * jax.nn.sigmoid(y_s @ w_glu + b_glu)
        merged = jax.nn.sigmoid(ga) * (a @ w_up_a) + jax.nn.sigmoid(gs) * (s @ w_up_s)
        x = x + g1 * (merged @ w_out)
        h2 = _rmsnorm(x, norm_ffn_g) * (1.0 + sc2) + sh2
        x = x + g2 * (jnp.square(jax.nn.relu(h2 @ w_ffn_up)) @ w_ffn_down)
        return x, k, v, lf, hT_re, hT_im

    def attend_prompt(q, k, v, lf):
        bsz, S = q.shape[0], q.shape[1]
        nblk = S // Q_BLOCK
        G = lax.cumsum(lf, axis=1)
        pos = jnp.arange(S)
        qb = q.reshape(bsz, nblk, Q_BLOCK, H_A, DH).transpose(1, 0, 2, 3, 4)
        gqb = G.reshape(bsz, nblk, Q_BLOCK, H_A).transpose(1, 0, 2, 3)
        posb = pos.reshape(nblk, Q_BLOCK)
        out = lax.map(lambda blk: _fox_attend(blk[0], k, v, blk[1], G, blk[2], pos), (qb, gqb, posb))
        return out.transpose(1, 0, 2, 3, 4).reshape(bsz, S, H_A, DH)

    h0p = jnp.zeros((x_prompt.shape[0], N_GROUPS, STATE_P), f32)
    y_prompt, k_p, v_p, lf_p, sre_p, sim_p = layer(x_prompt, c_prompt, attend_prompt, h0p, h0p)

    def attend_sample(q, k, v, lf):
        bsz, T = q.shape[0], q.shape[1]
        past = page_table.shape[1] * cache_k.shape[1]
        kp = cache_k[page_table].reshape(bsz, past, H_A, DH)
        vp = cache_v[page_table].reshape(bsz, past, H_A, DH)
        lfp = cache_logf[page_table].reshape(bsz, past, H_A).astype(f32)
        gk_past = lfp - lax.cumsum(lfp, axis=1, reverse=True)
        g_new = lax.cumsum(lf, axis=1)
        k_all = jnp.concatenate([kp.astype(k.dtype), k], axis=1)
        v_all = jnp.concatenate([vp.astype(v.dtype), v], axis=1)
        gk = jnp.concatenate([gk_past, g_new], axis=1)
        q_pos = past + jnp.arange(T)
        k_pos = jnp.arange(past + T)
        return _fox_attend(q, k_all, v_all, g_new, gk, q_pos, k_pos)

    y_sample, k_s, v_s, lf_s, sre_s, sim_s = layer(x_sample, c_sample, attend_sample, state_ssm_re, state_ssm_im)

    return (y_prompt, y_sample, k_p, v_p, lf_p, sre_p, sim_p, k_s, v_s, lf_s, sre_s, sim_s)
```

```python
import functools
import math

import jax
import jax.numpy as jnp
from jax import lax
from jax.experimental import pallas as pl
from jax.experimental.pallas import tpu as pltpu

F32 = jnp.float32
BF16 = jnp.bfloat16

D_MODEL = 1024
N_HEADS = 8
D_HEAD = 64
A_WIDTH = N_HEADS * D_HEAD
S_WIDTH = D_MODEL // 2
GROUP_CH = 16
N_GROUPS = S_WIDTH // GROUP_CH
STATE_P = 64
N_STATE = N_GROUPS * STATE_P
D_FF = 4 * D_MODEL
PAGE = 128
ATTN_SCALE = 1.0 / math.sqrt(D_HEAD)
NORM_EPS = 1e-6
NEG_BIG = -1e30

LANES = 128
HEAD_PAD = 128
QA_WIDTH = N_HEADS * HEAD_PAD
N_SPLIT = 3
VMEM_LIMIT = 56 * 1024 * 1024

_OQ, _OK, _OV, _OU, _OGA, _OGS, _OF, _W2 = 0, 512, 1024, 1536, 2048, 3072, 4096, 4224


def _cparams(*sem):
    return pltpu.CompilerParams(dimension_semantics=sem, vmem_limit_bytes=VMEM_LIMIT)


def _const_spec(shape):
    nd = len(shape)
    return pl.BlockSpec(shape, lambda *_: (0,) * nd, pipeline_mode=pl.Buffered(1))


def _split3(x):
    hi = x.astype(BF16)
    r1 = x - hi.astype(F32)
    mid = r1.astype(BF16)
    lo = (r1 - mid.astype(F32)).astype(BF16)
    return hi, mid, lo


def _dot(a, b):
    return jnp.dot(a, b, preferred_element_type=F32)


def _dot_nt(a, b):
    return lax.dot_general(a, b, (((1,), (1,)), ((), ())), preferred_element_type=F32)


def _adaln_kernel(c_ref, w_ref, b_ref, o_ref):
    c = c_ref[...]
    s = (c * jax.nn.sigmoid(c)).astype(BF16)
    o_ref[...] = _dot(s, w_ref[...]) + b_ref[...]


def _adaln(c_all, w_ada_bf, b_ada):
    n = c_all.shape[0]
    tn = 1024
    return pl.pallas_call(
        _adaln_kernel,
        grid=(6 * D_MODEL // tn,),
        in_specs=[pl.BlockSpec((n, D_MODEL), lambda j: (0, 0)),
                  pl.BlockSpec((D_MODEL, tn), lambda j: (0, j)),
                  pl.BlockSpec((1, tn), lambda j: (0, j))],
        out_specs=pl.BlockSpec((n, tn), lambda j: (0, j)),
        out_shape=jax.ShapeDtypeStruct((n, 6 * D_MODEL), F32),
        compiler_params=_cparams("parallel"),
        name="adaln",
    )(c_all, w_ada_bf, b_ada.reshape(1, -1))


def _s5_disc_kernel(lr_ref, li_ref, ldt_ref, br_ref, bi_ref, abr_ref, abi_ref, bbr_ref, bbi_ref):
    lr = lr_ref[...]
    li = li_ref[...]
    dt = jnp.exp(ldt_ref[...])
    mag = jnp.exp(lr * dt)
    ab_re = mag * jnp.cos(li * dt)
    ab_im = mag * jnp.sin(li * dt)
    nr, ni = ab_re - 1.0, ab_im
    den = lr * lr + li * li
    f_re = (nr * lr + ni * li) / den
    f_im = (ni * lr - nr * li) / den
    br = br_ref[...]
    bi = bi_ref[...]
    abr_ref[...] = ab_re
    abi_ref[...] = ab_im
    bbr_ref[...] = f_re * br - f_im * bi
    bbi_ref[...] = f_re * bi + f_im * br


def _s5_disc(lam_re, lam_im, log_dt, b_re, b_im):
    g3 = jax.ShapeDtypeStruct((N_GROUPS, 1, STATE_P), F32)
    b3 = jax.ShapeDtypeStruct((N_GROUPS, GROUP_CH, STATE_P), F32)
    return pl.pallas_call(
        _s5_disc_kernel,
        out_shape=(g3, g3, b3, b3),
        name="s5_disc",
    )(lam_re.reshape(N_GROUPS, 1, STATE_P), lam_im.reshape(N_GROUPS, 1, STATE_P),
      log_dt.reshape(N_GROUPS, 1, 1),
      jnp.swapaxes(b_re, 1, 2), jnp.swapaxes(b_im, 1, 2))


def _inproj_kernel(x_ref, sh_ref, sc_ref, gmix_ref, w_ref, bf_ref, gq_ref, gk_ref,
                   e2k_ref, cq_ref, tri_ref,
                   qa_ref, ka_ref, k_ref, v_ref, vb_ref, lf_ref, u_ref, sga_ref, sgs_ref,
                   *rest, bb, tt, emit_qn):
    if emit_qn:
        qn_ref, carry_ref = rest
    else:
        (carry_ref,) = rest
    rows = bb * tt
    t = pl.program_id(1)

    x = x_ref[...]
    ms = jnp.mean(x * x, axis=-1, keepdims=True)
    h = x * lax.rsqrt(ms + NORM_EPS) * gmix_ref[...]
    h = h * (1.0 + sc_ref[...]) + sh_ref[...]
    hb = h.reshape(rows, D_MODEL).astype(BF16)

    lane = lax.broadcasted_iota(jnp.int32, (1, LANES), 1)
    low_half = lane < D_HEAD

    zf = _dot(hb, w_ref[:, _OF:_W2]) + bf_ref[...]
    lf = jnp.minimum(zf, 0.0) - jnp.log1p(jnp.exp(-jnp.abs(zf)))
    lf = jnp.where(lane < N_HEADS, lf, 0.0)
    lf_ref[...] = lf[:, :N_HEADS].reshape(lf_ref.shape)
    hi, mid, lo = _split3(lf)
    packed = (hi.astype(F32) + pltpu.roll(mid.astype(F32), N_HEADS, 1)
              + pltpu.roll(lo.astype(F32), 2 * N_HEADS, 1)).astype(BF16)
    gc = _dot(tri_ref[...], packed)
    g = gc + pltpu.roll(gc, LANES - N_HEADS, 1) + pltpu.roll(gc, LANES - 2 * N_HEADS, 1)
    g = jnp.where(lane < N_HEADS, g, 0.0)
    if bb == 1:
        @pl.when(t == 0)
        def _():
            carry_ref[...] = jnp.zeros_like(carry_ref)
        g = g + carry_ref[...]
        carry_ref[...] = g[rows - 1:rows, :]
    ghi, gmid, glo = _split3(g)
    gpacked = (ghi.astype(F32) + pltpu.roll(gmid.astype(F32), N_HEADS, 1)
               + pltpu.roll(glo.astype(F32), 2 * N_HEADS, 1)).astype(BF16)
    kaug = _dot(gpacked, e2k_ref[...])

    def widen_norm(z, gain):
        out = []
        for part in (z, pltpu.roll(z, D_HEAD, 1)):
            e = jnp.where(low_half, part, 0.0)
            ss = jnp.sum(e * e, axis=-1, keepdims=True)
            out.append(e * lax.rsqrt(ss * (1.0 / D_HEAD) + NORM_EPS) * gain)
        return out

    gq = gq_ref[...] * ATTN_SCALE
    gk = gk_ref[...]
    for j in range(N_HEADS // 2):
        zq = _dot(hb, w_ref[:, _OQ + LANES * j:_OQ + LANES * (j + 1)])
        zk = _dot(hb, w_ref[:, _OK + LANES * j:_OK + LANES * (j + 1)])
        q_even, q_odd = widen_norm(zq, gq)
        k_even, k_odd = widen_norm(zk, gk)
        for i, (qh, kh) in enumerate(((q_even, k_even), (q_odd, k_odd))):
            c0 = HEAD_PAD * (2 * j + i)
            qa_ref[..., c0:c0 + HEAD_PAD] = (qh + cq_ref[:, c0:c0 + HEAD_PAD]).astype(BF16).reshape(bb, tt, HEAD_PAD)
            ka_ref[..., c0:c0 + HEAD_PAD] = (kh + kaug[:, c0:c0 + HEAD_PAD]).astype(BF16).reshape(bb, tt, HEAD_PAD)
        k_ref[..., LANES * j:LANES * (j + 1)] = (k_even + pltpu.roll(k_odd, D_HEAD, 1)).reshape(bb, tt, LANES)
        if emit_qn:
            qn_ref[..., LANES * j:LANES * (j + 1)] = (
                (q_even + pltpu.roll(q_odd, D_HEAD, 1)).astype(BF16).reshape(bb, tt, LANES))

    zv = _dot(hb, w_ref[:, _OV:_OU])
    v_ref[...] = zv.reshape(v_ref.shape)
    vb_ref[...] = zv.astype(BF16).reshape(vb_ref.shape)
    u_ref[...] = _dot(hb, w_ref[:, _OU:_OGA]).reshape(u_ref.shape)
    sga_ref[...] = jax.nn.sigmoid(_dot(hb, w_ref[:, _OGA:_OGS])).astype(BF16).reshape(sga_ref.shape)
    sgs_ref[...] = jax.nn.sigmoid(_dot(hb, w_ref[:, _OGS:_OF])).astype(BF16).reshape(sgs_ref.shape)


def _inproj(x, sh1, sc1, gmix, w2, bfp, gqp, gkp, e2k, cq, *, bb, tt, u_time_major, emit_qn):
    nb, nt_total, _ = x.shape
    rows = bb * tt
    idx = jnp.arange(rows)
    tri = ((idx[:, None] >= idx[None, :]) & (idx[:, None] // tt == idx[None, :] // tt)).astype(BF16)
    grid = (nb // bb, nt_total // tt)
    assert bb == 1 or grid[1] == 1

    def tok(width):
        return pl.BlockSpec((bb, tt, width), lambda b, t: (b, t, 0))

    def mod():
        return pl.BlockSpec((bb, 1, D_MODEL), lambda b, t: (b, 0, 0))

    if u_time_major:
        assert bb == 1
        u_shape = jax.ShapeDtypeStruct((nt_total, nb * S_WIDTH), F32)
        u_spec = pl.BlockSpec((tt, S_WIDTH), lambda b, t: (t, b))
    else:
        u_shape = jax.ShapeDtypeStruct((nb, nt_total, S_WIDTH), F32)
        u_spec = tok(S_WIDTH)

    def sds(width, dt):
        return jax.ShapeDtypeStruct((nb, nt_total, width), dt)

    out_shape = [sds(QA_WIDTH, BF16), sds(QA_WIDTH, BF16), sds(A_WIDTH, F32), sds(A_WIDTH, F32),
                 sds(A_WIDTH, BF16), sds(N_HEADS, F32), u_shape, sds(D_MODEL, BF16), sds(D_MODEL, BF16)]
    out_specs = [tok(QA_WIDTH), tok(QA_WIDTH), tok(A_WIDTH), tok(A_WIDTH), tok(A_WIDTH),
                 tok(N_HEADS), u_spec, tok(D_MODEL), tok(D_MODEL)]
    if emit_qn:
        out_shape.append(sds(A_WIDTH, BF16))
        out_specs.append(tok(A_WIDTH))

    return pl.pallas_call(
        functools.partial(_inproj_kernel, bb=bb, tt=tt, emit_qn=emit_qn),
        grid=grid,
        in_specs=[tok(D_MODEL), mod(), mod(), _const_spec((1, D_MODEL)), _const_spec((D_MODEL, _W2)),
                  _const_spec((1, LANES)), _const_spec((1, LANES)), _const_spec((1, LANES)),
                  _const_spec((LANES, QA_WIDTH)), _const_spec((1, QA_WIDTH)), _const_spec((rows, rows))],
        out_specs=out_specs,
        out_shape=out_shape,
        scratch_shapes=[pltpu.VMEM((1, LANES), F32)],
        compiler_params=_cparams("parallel", "arbitrary"),
        name="inproj",
    )(x, sh1, sc1, gmix, w2, bfp, gqp, gkp, e2k, cq, tri)


def _attn_kernel(qa_ref, ka_ref, vb_ref, o_ref, *, blk):
    qi = pl.program_id(1)
    row = lax.broadcasted_iota(jnp.int32, (blk, blk), 0)
    col = lax.broadcasted_iota(jnp.int32, (blk, blk), 1)
    causal = row >= col

    for hp in range(N_HEADS // 2):
        outs = []
        for h in (2 * hp, 2 * hp + 1):
            q = qa_ref[0, :, HEAD_PAD * h:HEAD_PAD * (h + 1)]

            def block(kb, carry, masked, h=h, q=q):
                m, l, acc = carry
                ks = pl.multiple_of(kb * blk, blk)
                k = ka_ref[0, pl.ds(ks, blk), HEAD_PAD * h:HEAD_PAD * (h + 1)]
                v = vb_ref[0, pl.ds(ks, blk), D_HEAD * h:D_HEAD * (h + 1)]
                s = _dot_nt(q, k)
                if masked:
                    s = jnp.where(causal, s, NEG_BIG)
                m_new = jnp.maximum(m, jnp.max(s, axis=-1, keepdims=True))
                alpha = jnp.exp(m - m_new)
                p = jnp.exp(s - m_new)
                l = alpha * l + jnp.sum(p, axis=-1, keepdims=True)
                acc = alpha * acc + _dot(p.astype(BF16), v)
                return m_new, l, acc

            init = (jnp.full((blk, 1), NEG_BIG, F32), jnp.zeros((blk, 1), F32),
                    jnp.zeros((blk, D_HEAD), F32))
            carry = lax.fori_loop(0, qi, functools.partial(block, masked=False), init)
            _, l, acc = block(qi, carry, masked=True)
            outs.append(acc / l)
        o_ref[0, :, LANES * hp:LANES * (hp + 1)] = jnp.concatenate(outs, axis=-1).astype(BF16)


def _attn(qa, ka, vb, *, blk):
    nb, nt, _ = qa.shape
    return pl.pallas_call(
        functools.partial(_attn_kernel, blk=blk),
        grid=(nb, nt // blk),
        in_specs=[pl.BlockSpec((1, blk, QA_WIDTH), lambda b, i: (b, i, 0)),
                  pl.BlockSpec((1, nt, QA_WIDTH), lambda b, i: (b, 0, 0)),
                  pl.BlockSpec((1, nt, A_WIDTH), lambda b, i: (b, 0, 0))],
        out_specs=pl.BlockSpec((1, blk, A_WIDTH), lambda b, i: (b, i, 0)),
        out_shape=jax.ShapeDtypeStruct((nb, nt, A_WIDTH), BF16),
        compiler_params=_cparams("parallel", "arbitrary"),
        name="attn",
    )(qa, ka, vb)


def _pool_logf_kernel(lf_ref, m_ref, o_ref):
    hi, mid, lo = _split3(lf_ref[...])
    m = m_ref[...]
    o_ref[...] = _dot(hi, m) + _dot(mid, m) + _dot(lo, m)


def _pool_logf(lf_rows):
    n_rows = lf_rows.shape[0]
    tile = next(c for c in (2048, 1024, 512, 256, 128, 64, 32, 16, 8) if n_rows % c == 0)
    j_src = jnp.arange(PAGE)
    suffix = j_src[:, None] > j_src[None, :]
    m = jnp.concatenate([suffix, jnp.ones((PAGE, PAGE), bool)], axis=1).astype(BF16)
    return pl.pallas_call(
        _pool_logf_kernel,
        grid=(n_rows // tile,),
        in_specs=[pl.BlockSpec((tile, PAGE), lambda i: (i, 0)), _const_spec((PAGE, 2 * PAGE))],
        out_specs=pl.BlockSpec((tile, 2 * PAGE), lambda i: (i, 0)),
        out_shape=jax.ShapeDtypeStruct((n_rows, 2 * PAGE), F32),
        compiler_params=_cparams("parallel"),
        name="pool_logf",
    )(lf_rows, m)


def _sattn_kernel(pt_ref, qbd_ref, qabd_ref, kan_ref, vn_ref, *rest, pp, ts):
    k_refs, v_refs, r_refs = rest[:pp], rest[pp:2 * pp], rest[2 * pp:3 * pp]
    o_ref, kbuf, vbuf, m_sc, l_sc, acc_sc, tot_sc = rest[3 * pp:]
    del pt_ref
    j = pl.program_id(1)
    nrow = N_HEADS * ts

    @pl.when(j == 0)
    def _():
        m_sc[...] = jnp.full_like(m_sc, NEG_BIG)
        l_sc[...] = jnp.zeros_like(l_sc)
        acc_sc[...] = jnp.zeros_like(acc_sc)
        tot_sc[...] = jnp.zeros_like(tot_sc)

    for i in range(pp):
        kbuf[:, PAGE * i:PAGE * (i + 1)] = k_refs[i][0].astype(BF16)
        vbuf[:, PAGE * i:PAGE * (i + 1)] = v_refs[i][0].astype(BF16)

    def update(s, pv):
        m_old = m_sc[...]
        m_new = jnp.maximum(m_old, jnp.max(s, axis=-1, keepdims=True))
        alpha = jnp.exp(m_old - m_new)
        p = jnp.exp(s - m_new)
        l_sc[...] = alpha * l_sc[...] + jnp.sum(p, axis=-1, keepdims=True)
        acc_sc[...] = alpha * acc_sc[...] + pv(p.astype(BF16))
        m_sc[...] = m_new

    later = tot_sc[...]
    biases = []
    for i in range(pp):
        r = r_refs[i][0]
        biases.append(jnp.concatenate([r[:, :PAGE] + later] * ts, axis=0))
        later = later + r[:, PAGE:]
    tot_sc[...] = later
    s = _dot(qbd_ref[0], kbuf[...]) + jnp.concatenate(biases, axis=1)
    update(s, lambda p: _dot_nt(p, vbuf[...]))

    @pl.when(j == pl.num_programs(1) - 1)
    def _():
        s_new = _dot_nt(qabd_ref[0], kan_ref[0])
        r_i = lax.broadcasted_iota(jnp.int32, (nrow, ts), 0)
        c_i = lax.broadcasted_iota(jnp.int32, (nrow, ts), 1)
        update(jnp.where((r_i // N_HEADS) >= c_i, s_new, NEG_BIG), lambda p: _dot(p, vn_ref[0]))
        o = acc_sc[...] / l_sc[...]
        lane_head = lax.broadcasted_iota(jnp.int32, (N_HEADS, A_WIDTH), 1) // D_HEAD
        own = lane_head == lax.broadcasted_iota(jnp.int32, (N_HEADS, A_WIDTH), 0)
        rows = [jnp.sum(jnp.where(own, o[N_HEADS * t:N_HEADS * (t + 1), :], 0.0), axis=0, keepdims=True)
                for t in range(ts)]
        o_ref[0] = jnp.concatenate(rows, axis=0).astype(BF16)


def _sattn(page_table, qbd, qabd, kan, vn, cache_kt, cache_vt, rsuf, *, pp):
    nb, n_pages = page_table.shape
    ts = kan.shape[1]
    nrow = N_HEADS * ts
    steps = n_pages // pp
    pt_flat = page_table.reshape(-1)

    def page_spec(i, shape):
        def imap(b, j, pt):
            return (pt[b * n_pages + (n_pages - 1 - (j * pp + i))], 0, 0)
        return pl.BlockSpec(shape, imap)

    def per_seq(shape):
        return pl.BlockSpec(shape, lambda b, j, pt: (b, 0, 0))

    in_specs = ([per_seq((1, nrow, A_WIDTH)), per_seq((1, nrow, QA_WIDTH)),
                 per_seq((1, ts, QA_WIDTH)), per_seq((1, ts, A_WIDTH))]
                + [page_spec(i, (1, A_WIDTH, PAGE)) for i in range(pp)]
                + [page_spec(i, (1, A_WIDTH, PAGE)) for i in range(pp)]
                + [page_spec(i, (1, N_HEADS, 2 * PAGE)) for i in range(pp)])
    grid_spec = pltpu.PrefetchScalarGridSpec(
        num_scalar_prefetch=1,
        grid=(nb, steps),
        in_specs=in_specs,
        out_specs=per_seq((1, ts, A_WIDTH)),
        scratch_shapes=[pltpu.VMEM((A_WIDTH, pp * PAGE), BF16), pltpu.VMEM((A_WIDTH, pp * PAGE), BF16),
                        pltpu.VMEM((nrow, 1), F32), pltpu.VMEM((nrow, 1), F32),
                        pltpu.VMEM((nrow, A_WIDTH), F32), pltpu.VMEM((N_HEADS, PAGE), F32)],
    )
    return pl.pallas_call(
        functools.partial(_sattn_kernel, pp=pp, ts=ts),
        grid_spec=grid_spec,
        out_shape=jax.ShapeDtypeStruct((nb, ts, A_WIDTH), BF16),
        compiler_params=_cparams("parallel", "arbitrary"),
        name="sattn",
    )(pt_flat, qbd, qabd, kan, vn, *([cache_kt] * pp), *([cache_vt] * pp), *([rsuf] * pp))


def _gelu_tanh(x):
    return 0.5 * x * (1.0 + jnp.tanh(math.sqrt(2.0 / math.pi) * (x + 0.044715 * (x * x * x))))


def _s5_kernel(u_ref, h0_ref, a_ref, bmat_ref, cmat_ref, d_ref, y_ref, ht_ref, hist, state,
               *, tc, bs, rb, lc):
    t = pl.program_id(0)

    @pl.when(t == 0)
    def _():
        state[...] = h0_ref[...]

    u = u_ref[...].reshape(tc * bs, S_WIDTH)
    hist[...] = _dot(u.astype(BF16), bmat_ref[...])

    for c in range(N_STATE // lc):
        re = slice(lc * c, lc * (c + 1))
        im = slice(N_STATE + lc * c, N_STATE + lc * (c + 1))
        a_re = jnp.broadcast_to(a_ref[:, re], (rb, lc))
        a_im = jnp.broadcast_to(a_ref[:, im], (rb, lc))

        def row_block(r, _, re=re, im=im, a_re=a_re, a_im=a_im):
            r0 = pl.multiple_of(r * rb, rb)

            def step(j, carry):
                h_re, h_im = carry
                row = pl.multiple_of(j * bs + r0, rb)
                n_re = a_re * h_re - a_im * h_im + hist[pl.ds(row, rb), re]
                n_im = a_re * h_im + a_im * h_re + hist[pl.ds(row, rb), im]
                hist[pl.ds(row, rb), re] = n_re
                hist[pl.ds(row, rb), im] = n_im
                return n_re, n_im

            h_re, h_im = lax.fori_loop(0, tc, step, (state[pl.ds(r0, rb), re], state[pl.ds(r0, rb), im]))
            state[pl.ds(r0, rb), re] = h_re
            state[pl.ds(r0, rb), im] = h_im
            return 0

        lax.fori_loop(0, bs // rb, row_block, 0)

    y = _dot(hist[...].astype(BF16), cmat_ref[...]) + d_ref[...] * u
    y_ref[...] = _gelu_tanh(y).astype(BF16).reshape(y_ref.shape)
    ht_ref[...] = state[...]


def _s5(u_tm, h0, a_row, bmat, cmat, d_row, *, tc):
    nt, bs, _ = u_tm.shape
    rb = 16 if bs % 16 == 0 else 8
    assert bs % rb == 0 and nt % tc == 0
    return pl.pallas_call(
        functools.partial(_s5_kernel, tc=tc, bs=bs, rb=rb, lc=256),
        grid=(nt // tc,),
        in_specs=[pl.BlockSpec((tc, bs, S_WIDTH), lambda t: (t, 0, 0)),
                  _const_spec((bs, 2 * N_STATE)), _const_spec((1, 2 * N_STATE)),
                  _const_spec((S_WIDTH, 2 * N_STATE)), _const_spec((2 * N_STATE, S_WIDTH)),
                  _const_spec((1, S_WIDTH))],
        out_specs=[pl.BlockSpec((tc, bs, S_WIDTH), lambda t: (t, 0, 0)),
                   pl.BlockSpec((bs, 2 * N_STATE), lambda t: (0, 0))],
        out_shape=[jax.ShapeDtypeStruct((nt, bs, S_WIDTH), BF16),
                   jax.ShapeDtypeStruct((bs, 2 * N_STATE), F32)],
        scratch_shapes=[pltpu.VMEM((tc * bs, 2 * N_STATE), F32), pltpu.VMEM((bs, 2 * N_STATE), F32)],
        compiler_params=_cparams("arbitrary"),
        name="s5",
    )(u_tm, h0, a_row, bmat, cmat, d_row)


def _post_kernel(x_ref, a_ref, ys_ref, sga_ref, sgs_ref, g1_ref, sh2_ref, sc2_ref, g2_ref, gffn_ref,
                 wglu_ref, bglu_ref, wupa_ref, wups_ref, wout_ref, wffu_ref, wffd_ref, y_ref,
                 *, bb, tt, ffc):
    rows = bb * tt
    ys = ys_ref[...].reshape(rows, S_WIDTH)
    glu = _dot(ys, wglu_ref[...]) + bglu_ref[...]
    s = (ys.astype(F32) * jax.nn.sigmoid(glu)).astype(BF16)
    a = a_ref[...].reshape(rows, A_WIDTH)
    merged = (sga_ref[...].reshape(rows, D_MODEL).astype(F32) * _dot(a, wupa_ref[...])
              + sgs_ref[...].reshape(rows, D_MODEL).astype(F32) * _dot(s, wups_ref[...]))
    mix = _dot(merged.astype(BF16), wout_ref[...]).reshape(bb, tt, D_MODEL)
    x1 = x_ref[...] + g1_ref[...] * mix

    ms = jnp.mean(x1 * x1, axis=-1, keepdims=True)
    h2 = x1 * lax.rsqrt(ms + NORM_EPS) * gffn_ref[...]
    h2 = (h2 * (1.0 + sc2_ref[...]) + sh2_ref[...]).reshape(rows, D_MODEL).astype(BF16)
    down = jnp.zeros((rows, D_MODEL), F32)
    for c in range(D_FF // ffc):
        up = jnp.maximum(_dot(h2, wffu_ref[:, ffc * c:ffc * (c + 1)]), 0.0)
        down = down + _dot((up * up).astype(BF16), wffd_ref[ffc * c:ffc * (c + 1), :])
    y_ref[...] = x1 + g2_ref[...] * down.reshape(bb, tt, D_MODEL)


def _post(x, a, ys, sga, sgs, g1, sh2, sc2, g2, gffn, wglu, bglu, wupa, wups, wout, wffu, wffd,
          *, bb, tt, ys_time_major):
    nb, nt_total, _ = x.shape

    def tok(width):
        return pl.BlockSpec((bb, tt, width), lambda b, t: (b, t, 0))

    def mod():
        return pl.BlockSpec((bb, 1, D_MODEL), lambda b, t: (b, 0, 0))

    if ys_time_major:
        assert bb == 1
        ys_spec = pl.BlockSpec((tt, S_WIDTH), lambda b, t: (t, b))
    else:
        ys_spec = tok(S_WIDTH)

    return pl.pallas_call(
        functools.partial(_post_kernel, bb=bb, tt=tt, ffc=1024),
        grid=(nb // bb, nt_total // tt),
        in_specs=[tok(D_MODEL), tok(A_WIDTH), ys_spec, tok(D_MODEL), tok(D_MODEL),
                  mod(), mod(), mod(), mod(), _const_spec((1, D_MODEL)),
                  _const_spec((S_WIDTH, S_WIDTH)), _const_spec((1, S_WIDTH)),
                  _const_spec((A_WIDTH, D_MODEL)), _const_spec((S_WIDTH, D_MODEL)),
                  _const_spec((D_MODEL, D_MODEL)), _const_spec((D_MODEL, D_FF)), _const_spec((D_FF, D_MODEL))],
        out_specs=tok(D_MODEL),
        out_shape=jax.ShapeDtypeStruct((nb, nt_total, D_MODEL), F32),
        compiler_params=_cparams("parallel", "parallel"),
        name="post",
    )(x, a, ys, sga, sgs, g1, sh2, sc2, g2, gffn, wglu, bglu, wupa, wups, wout, wffu, wffd)


def _block_diag_heads(q, width):
    nb, ts, _ = q.shape
    q5 = q.reshape(nb, ts, 1, N_HEADS, width)
    keep = jnp.eye(N_HEADS, dtype=bool).reshape(1, 1, N_HEADS, N_HEADS, 1)
    return jnp.where(keep, q5, jnp.zeros((), q.dtype)).reshape(nb, ts * N_HEADS, N_HEADS * width)


def kernel(x_prompt, x_sample, cache_k, cache_v, cache_logf, state_ssm_re, state_ssm_im, page_table,
           c_prompt, c_sample, w_ada, b_ada, norm_mix_g, norm_ffn_g, w_in, b_fgate, q_norm_g, k_norm_g,
           ssm_lambda_re, ssm_lambda_im, ssm_log_dt, ssm_b_re, ssm_b_im, ssm_c_re, ssm_c_im, ssm_d,
           w_glu, b_glu, w_up_a, w_up_s, w_out, w_ffn_up, w_ffn_down):
    nbp, ntp, _ = x_prompt.shape
    nbs, nts, _ = x_sample.shape
    n_pool = cache_k.shape[0]

    a3 = 3 * A_WIDTH
    wf_pad = jnp.pad(w_in[:, a3:a3 + N_HEADS], ((0, 0), (0, LANES - N_HEADS)))
    w2 = jnp.concatenate([w_in[:, :a3], w_in[:, a3 + N_HEADS:], wf_pad], axis=1).astype(BF16)
    bfp = jnp.pad(b_fgate, (0, LANES - N_HEADS)).reshape(1, LANES)
    gqp = jnp.pad(q_norm_g, (0, LANES - D_HEAD)).reshape(1, LANES)
    gkp = jnp.pad(k_norm_g, (0, LANES - D_HEAD)).reshape(1, LANES)
    heads = jnp.arange(N_HEADS)
    e2k = jnp.zeros((LANES, QA_WIDTH), F32)
    cq = jnp.zeros((1, QA_WIDTH), F32)
    for piece in range(N_SPLIT):
        e2k = e2k.at[piece * N_HEADS + heads, HEAD_PAD * heads + D_HEAD + piece].set(-1.0)
        cq = cq.at[0, HEAD_PAD * heads + D_HEAD + piece].set(1.0)
    e2k = e2k.astype(BF16)
    gmix = norm_mix_g.reshape(1, D_MODEL)
    gffn = norm_ffn_g.reshape(1, D_MODEL)
    post_w = (w_glu.astype(BF16), b_glu.reshape(1, S_WIDTH), w_up_a.astype(BF16), w_up_s.astype(BF16),
              w_out.astype(BF16), w_ffn_up.astype(BF16), w_ffn_down.astype(BF16))

    mod = _adaln(jnp.concatenate([c_prompt, c_sample], axis=0), w_ada.astype(BF16), b_ada)
    mod = mod.reshape(nbp + nbs, 1, 6, D_MODEL)
    mods_p = [mod[:nbp, :, i, :] for i in range(6)]
    mods_s = [mod[nbp:, :, i, :] for i in range(6)]

    ab_re, ab_im, bb_re, bb_im = _s5_disc(ssm_lambda_re, ssm_lambda_im, ssm_log_dt, ssm_b_re, ssm_b_im)
    a_row = jnp.concatenate([ab_re.reshape(1, N_STATE), ab_im.reshape(1, N_STATE)], axis=1)
    eye_g = jnp.eye(N_GROUPS, dtype=F32)

    def b_blockdiag(bb):
        return jnp.einsum('gcp,gh->gchp', bb, eye_g).reshape(S_WIDTH, N_STATE)

    def c_blockdiag(cc):
        return jnp.einsum('gcp,gh->gphc', cc, eye_g).reshape(N_STATE, S_WIDTH)

    bmat = jnp.concatenate([b_blockdiag(bb_re), b_blockdiag(bb_im)], axis=1).astype(BF16)
    cmat = jnp.concatenate([c_blockdiag(ssm_c_re), -c_blockdiag(ssm_c_im)], axis=0).astype(BF16)
    d_row = ssm_d.reshape(1, S_WIDTH)

    tt = min(512, ntp)
    qa, ka, k_p, v_p, vb, lf_p, u_tm, sga, sgs = _inproj(
        x_prompt, mods_p[0], mods_p[1], gmix, w2, bfp, gqp, gkp, e2k, cq,
        bb=1, tt=tt, u_time_major=True, emit_qn=False)
    attn_p = _attn(qa, ka, vb, blk=min(256, ntp))
    h0 = jnp.zeros((nbp, 2 * N_STATE), F32)
    ys_tm, ht_p = _s5(u_tm.reshape(ntp, nbp, S_WIDTH), h0, a_row, bmat, cmat, d_row, tc=min(32, ntp))
    y_prompt = _post(x_prompt, attn_p, ys_tm.reshape(ntp, nbp * S_WIDTH), sga, sgs,
                     mods_p[2], mods_p[3], mods_p[4], mods_p[5], gffn, *post_w,
                     bb=1, tt=tt, ys_time_major=True)

    bbs = min(64, nbs)
    qa_s, ka_s, k_s, v_s, vb_s, lf_s, u_s, sga_s, sgs_s, qn_s = _inproj(
        x_sample, mods_s[0], mods_s[1], gmix, w2, bfp, gqp, gkp, e2k, cq,
        bb=bbs, tt=nts, u_time_major=False, emit_qn=True)
    lf_rows = jnp.swapaxes(cache_logf, 1, 2).reshape(n_pool * N_HEADS, PAGE)
    rsuf = _pool_logf(lf_rows).reshape(n_pool, N_HEADS, 2 * PAGE)
    cache_kt = jnp.transpose(cache_k, (0, 2, 3, 1)).reshape(n_pool, A_WIDTH, PAGE)
    cache_vt = jnp.transpose(cache_v, (0, 2, 3, 1)).reshape(n_pool, A_WIDTH, PAGE)
    attn_s = _sattn(page_table, _block_diag_heads(qn_s, D_HEAD), _block_diag_heads(qa_s, HEAD_PAD),
                    ka_s, vb_s, cache_kt, cache_vt, rsuf, pp=min(8, page_table.shape[1]))
    h0_s = jnp.concatenate([state_ssm_re.reshape(nbs, N_STATE), state_ssm_im.reshape(nbs, N_STATE)], axis=1)
    ys_s_tm, ht_s = _s5(jnp.swapaxes(u_s, 0, 1), h0_s, a_row, bmat, cmat, d_row, tc=nts)
    y_sample = _post(x_sample, attn_s, jnp.swapaxes(ys_s_tm, 0, 1), sga_s, sgs_s,
                     mods_s[2], mods_s[3], mods_s[4], mods_s[5], gffn, *post_w,
                     bb=bbs, tt=nts, ys_time_major=False)

    def heads4(z):
        return z.reshape(z.shape[0], z.shape[1], N_HEADS, D_HEAD)

    def state3(z):
        return z.reshape(z.shape[0], N_GROUPS, STATE_P)

    return (y_prompt, y_sample, heads4(k_p), heads4(v_p), lf_p,
            state3(ht_p[:, :N_STATE]), state3(ht_p[:, N_STATE:]),
            heads4(k_s), heads4(v_s), lf_s,
            state3(ht_s[:, :N_STATE]), state3(ht_s[:, N_STATE:]))
```

```python
import functools
import math

import jax
import jax.numpy as jnp
from jax import lax
from jax.experimental import pallas as pl
from jax.experimental.pallas import tpu as pltpu

F32 = jnp.float32
BF16 = jnp.bfloat16

D_MODEL = 1024
N_HEADS = 8
D_HEAD = 64
A_WIDTH = N_HEADS * D_HEAD
S_WIDTH = D_MODEL // 2
GROUP_CH = 16
N_GROUPS = S_WIDTH // GROUP_CH
STATE_P = 64
N_STATE = N_GROUPS * STATE_P
D_FF = 4 * D_MODEL
PAGE = 128
ATTN_SCALE = 1.0 / math.sqrt(D_HEAD)
NORM_EPS = 1e-6
NEG_BIG = -1e30

LANES = 128
MXU_TILE = 256
HEAD_PAD = 128
QA_WIDTH = N_HEADS * HEAD_PAD
N_SPLIT = 3
VMEM_LIMIT = 56 * 1024 * 1024

_OQ, _OK, _OV, _OU, _OGA, _OGS, _OF, _W2 = 0, 512, 1024, 1536, 2048, 3072, 4096, 4224


def _cparams(*sem):
    return pltpu.CompilerParams(dimension_semantics=sem, vmem_limit_bytes=VMEM_LIMIT)


def _const_spec(shape):
    nd = len(shape)
    return pl.BlockSpec(shape, lambda *_: (0,) * nd, pipeline_mode=pl.Buffered(1))


def _split3(x):
    hi = x.astype(BF16)
    r1 = x - hi.astype(F32)
    mid = r1.astype(BF16)
    lo = (r1 - mid.astype(F32)).astype(BF16)
    return hi, mid, lo


def _dot(a, b):
    return jnp.dot(a, b, preferred_element_type=F32)


def _dot_nt(a, b):
    return lax.dot_general(a, b, (((1,), (1,)), ((), ())), preferred_element_type=F32)


def _adaln_kernel(c_ref, w_ref, b_ref, o_ref):
    c = c_ref[...]
    s = (c * jax.nn.sigmoid(c)).astype(BF16)
    o_ref[...] = _dot(s, w_ref[...]) + b_ref[...]


def _adaln(c_all, w_ada_bf, b_ada):
    n = c_all.shape[0]
    tn = 1024
    return pl.pallas_call(
        _adaln_kernel,
        grid=(6 * D_MODEL // tn,),
        in_specs=[pl.BlockSpec((n, D_MODEL), lambda j: (0, 0)),
                  pl.BlockSpec((D_MODEL, tn), lambda j: (0, j)),
                  pl.BlockSpec((1, tn), lambda j: (0, j))],
        out_specs=pl.BlockSpec((n, tn), lambda j: (0, j)),
        out_shape=jax.ShapeDtypeStruct((n, 6 * D_MODEL), F32),
        compiler_params=_cparams("parallel"),
        name="adaln",
    )(c_all, w_ada_bf, b_ada.reshape(1, -1))


def _s5_disc_kernel(lr_ref, li_ref, ldt_ref, br_ref, bi_ref, abr_ref, abi_ref, bbr_ref, bbi_ref):
    lr = lr_ref[...]
    li = li_ref[...]
    dt = jnp.exp(ldt_ref[...])
    mag = jnp.exp(lr * dt)
    ab_re = mag * jnp.cos(li * dt)
    ab_im = mag * jnp.sin(li * dt)
    nr, ni = ab_re - 1.0, ab_im
    den = lr * lr + li * li
    f_re = (nr * lr + ni * li) / den
    f_im = (ni * lr - nr * li) / den
    br = br_ref[...]
    bi = bi_ref[...]
    abr_ref[...] = ab_re
    abi_ref[...] = ab_im
    bbr_ref[...] = f_re * br - f_im * bi
    bbi_ref[...] = f_re * bi + f_im * br


def _s5_disc(lam_re, lam_im, log_dt, b_re, b_im):
    g3 = jax.ShapeDtypeStruct((N_GROUPS, 1, STATE_P), F32)
    b3 = jax.ShapeDtypeStruct((N_GROUPS, GROUP_CH, STATE_P), F32)
    return pl.pallas_call(
        _s5_disc_kernel,
        out_shape=(g3, g3, b3, b3),
        name="s5_disc",
    )(lam_re.reshape(N_GROUPS, 1, STATE_P), lam_im.reshape(N_GROUPS, 1, STATE_P),
      log_dt.reshape(N_GROUPS, 1, 1),
      jnp.swapaxes(b_re, 1, 2), jnp.swapaxes(b_im, 1, 2))


def _inproj_kernel(x_ref, sh_ref, sc_ref, gmix_ref, w_ref, wt_ref, bf_ref, gq_ref, gqcol_ref, gk_ref,
                   e2k_ref, cq_ref, tri_ref, *rest, bb, tt, prompt, kblk):
    if prompt:
        qat_ref, ka_ref, k_ref, vt_ref, vtb_ref, lf_ref, u_ref, sga_ref, sgs_ref, carry_ref = rest
    else:
        qa_ref, ka_ref, k_ref, v_ref, vb_ref, lf_ref, u_ref, sga_ref, sgs_ref, qn_ref, carry_ref = rest
    rows = bb * tt
    t = pl.program_id(1)

    x = x_ref[...]
    ms = jnp.mean(x * x, axis=-1, keepdims=True)
    h = x * lax.rsqrt(ms + NORM_EPS) * gmix_ref[...]
    h = h * (1.0 + sc_ref[...]) + sh_ref[...]
    hb = h.reshape(rows, D_MODEL).astype(BF16)

    lane = lax.broadcasted_iota(jnp.int32, (1, LANES), 1)
    low_half = lane < D_HEAD

    zf = _dot(hb, w_ref[:, _OF:_W2]) + bf_ref[...]
    lf = jnp.minimum(zf, 0.0) - jnp.log1p(jnp.exp(-jnp.abs(zf)))
    lf = jnp.where(lane < N_HEADS, lf, 0.0)
    lf_ref[...] = lf[:, :N_HEADS].reshape(lf_ref.shape)
    hi, mid, lo = _split3(lf)
    packed = (hi.astype(F32) + pltpu.roll(mid.astype(F32), N_HEADS, 1)
              + pltpu.roll(lo.astype(F32), 2 * N_HEADS, 1)).astype(BF16)
    gc = _dot(tri_ref[...], packed)
    g = gc + pltpu.roll(gc, LANES - N_HEADS, 1) + pltpu.roll(gc, LANES - 2 * N_HEADS, 1)
    g = jnp.where(lane < N_HEADS, g, 0.0)
    if bb == 1:
        @pl.when(t == 0)
        def _():
            carry_ref[...] = jnp.zeros_like(carry_ref)
        g = g + carry_ref[...]
        carry_ref[...] = g[rows - 1:rows, :]
    ghi, gmid, glo = _split3(g)
    gpacked = (ghi.astype(F32) + pltpu.roll(gmid.astype(F32), N_HEADS, 1)
               + pltpu.roll(glo.astype(F32), 2 * N_HEADS, 1)).astype(BF16)
    kaug = _dot(gpacked, e2k_ref[...])

    def widen_norm(z, gain):
        out = []
        for part in (z, pltpu.roll(z, D_HEAD, 1)):
            e = jnp.where(low_half, part, 0.0)
            ss = jnp.sum(e * e, axis=-1, keepdims=True)
            out.append(e * lax.rsqrt(ss * (1.0 / D_HEAD) + NORM_EPS) * gain)
        return out

    gq = gq_ref[...] * ATTN_SCALE
    gk = gk_ref[...]
    for j in range(N_HEADS // 2):
        zk = _dot(hb, w_ref[:, _OK + LANES * j:_OK + LANES * (j + 1)])
        k_even, k_odd = widen_norm(zk, gk)
        for i, kh in enumerate((k_even, k_odd)):
            c0 = HEAD_PAD * (2 * j + i)
            ka_ref[..., c0:c0 + HEAD_PAD] = (kh + kaug[:, c0:c0 + HEAD_PAD]).astype(BF16).reshape(bb, tt, HEAD_PAD)
        k_ref[..., LANES * j:LANES * (j + 1)] = (k_even + pltpu.roll(k_odd, D_HEAD, 1)).reshape(bb, tt, LANES)
        if not prompt:
            zq = _dot(hb, w_ref[:, _OQ + LANES * j:_OQ + LANES * (j + 1)])
            q_even, q_odd = widen_norm(zq, gq)
            for i, qh in enumerate((q_even, q_odd)):
                c0 = HEAD_PAD * (2 * j + i)
                qa_ref[..., c0:c0 + HEAD_PAD] = (
                    (qh + cq_ref[:, c0:c0 + HEAD_PAD]).astype(BF16).reshape(bb, tt, HEAD_PAD))
            qn_ref[..., LANES * j:LANES * (j + 1)] = (
                (q_even + pltpu.roll(q_odd, D_HEAD, 1)).astype(BF16).reshape(bb, tt, LANES))

    if prompt:
        zqt = _dot_nt(wt_ref[:A_WIDTH, :], hb)
        gq_col = gqcol_ref[...] * ATTN_SCALE
        sub = lax.broadcasted_iota(jnp.int32, (D_HEAD, tt), 0)
        ones_rows = jnp.where(sub < N_SPLIT, 1.0, 0.0).astype(BF16)
        for hd in range(N_HEADS):
            z = zqt[D_HEAD * hd:D_HEAD * (hd + 1), :]
            ss = jnp.sum(z * z, axis=0, keepdims=True)
            qn_t = z * lax.rsqrt(ss * (1.0 / D_HEAD) + NORM_EPS) * gq_col
            qat_ref[0, HEAD_PAD * hd:HEAD_PAD * hd + D_HEAD, :] = qn_t.astype(BF16)
            qat_ref[0, HEAD_PAD * hd + D_HEAD:HEAD_PAD * (hd + 1), :] = ones_rows
        zvt = _dot_nt(wt_ref[A_WIDTH:, :], hb)
        vt_ref[0] = zvt
        for c in range(tt // kblk):
            vtb_ref[0, c] = zvt[:, kblk * c:kblk * (c + 1)].astype(BF16)
    else:
        zv = _dot(hb, w_ref[:, _OV:_OU])
        v_ref[...] = zv.reshape(v_ref.shape)
        vb_ref[...] = zv.astype(BF16).reshape(vb_ref.shape)
    u_ref[...] = _dot(hb, w_ref[:, _OU:_OGA]).reshape(u_ref.shape)
    sga_ref[...] = jax.nn.sigmoid(_dot(hb, w_ref[:, _OGA:_OGS])).astype(BF16).reshape(sga_ref.shape)
    sgs_ref[...] = jax.nn.sigmoid(_dot(hb, w_ref[:, _OGS:_OF])).astype(BF16).reshape(sgs_ref.shape)


def _inproj(x, sh1, sc1, gmix, w2, wt, bfp, gqp, gq_col, gkp, e2k, cq, *, bb, tt, prompt, kblk):
    nb, nt_total, _ = x.shape
    u_time_major = prompt
    rows = bb * tt
    idx = jnp.arange(rows)
    tri = ((idx[:, None] >= idx[None, :]) & (idx[:, None] // tt == idx[None, :] // tt)).astype(BF16)
    grid = (nb // bb, nt_total // tt)
    assert bb == 1 or grid[1] == 1

    def tok(width):
        return pl.BlockSpec((bb, tt, width), lambda b, t: (b, t, 0))

    def mod():
        return pl.BlockSpec((bb, 1, D_MODEL), lambda b, t: (b, 0, 0))

    if u_time_major:
        assert bb == 1
        u_shape = jax.ShapeDtypeStruct((nt_total, nb * S_WIDTH), F32)
        u_spec = pl.BlockSpec((tt, S_WIDTH), lambda b, t: (t, b))
    else:
        u_shape = jax.ShapeDtypeStruct((nb, nt_total, S_WIDTH), F32)
        u_spec = tok(S_WIDTH)

    def sds(width, dt):
        return jax.ShapeDtypeStruct((nb, nt_total, width), dt)

    tail_shape = [sds(N_HEADS, F32), u_shape, sds(D_MODEL, BF16), sds(D_MODEL, BF16)]
    tail_specs = [tok(N_HEADS), u_spec, tok(D_MODEL), tok(D_MODEL)]
    if prompt:
        assert tt % kblk == 0
        out_shape = [jax.ShapeDtypeStruct((nb, QA_WIDTH, nt_total), BF16), sds(QA_WIDTH, BF16), sds(A_WIDTH, F32),
                     jax.ShapeDtypeStruct((nb, A_WIDTH, nt_total), F32),
                     jax.ShapeDtypeStruct((nb, nt_total // kblk, A_WIDTH, kblk), BF16)] + tail_shape
        out_specs = [pl.BlockSpec((1, QA_WIDTH, tt), lambda b, t: (b, 0, t)), tok(QA_WIDTH), tok(A_WIDTH),
                     pl.BlockSpec((1, A_WIDTH, tt), lambda b, t: (b, 0, t)),
                     pl.BlockSpec((1, tt // kblk, A_WIDTH, kblk), lambda b, t: (b, t, 0, 0))] + tail_specs
    else:
        out_shape = ([sds(QA_WIDTH, BF16), sds(QA_WIDTH, BF16), sds(A_WIDTH, F32), sds(A_WIDTH, F32),
                      sds(A_WIDTH, BF16)] + tail_shape + [sds(A_WIDTH, BF16)])
        out_specs = ([tok(QA_WIDTH), tok(QA_WIDTH), tok(A_WIDTH), tok(A_WIDTH), tok(A_WIDTH)]
                     + tail_specs + [tok(A_WIDTH)])

    return pl.pallas_call(
        functools.partial(_inproj_kernel, bb=bb, tt=tt, prompt=prompt, kblk=kblk),
        grid=grid,
        in_specs=[tok(D_MODEL), mod(), mod(), _const_spec((1, D_MODEL)), _const_spec((D_MODEL, _W2)),
                  _const_spec((2 * A_WIDTH, D_MODEL)),
                  _const_spec((1, LANES)), _const_spec((1, LANES)), _const_spec((D_HEAD, 1)), _const_spec((1, LANES)),
                  _const_spec((LANES, QA_WIDTH)), _const_spec((1, QA_WIDTH)), _const_spec((rows, rows))],
        out_specs=out_specs,
        out_shape=out_shape,
        scratch_shapes=[pltpu.VMEM((1, LANES), F32)],
        compiler_params=_cparams("parallel", "arbitrary"),
        name="inproj",
    )(x, sh1, sc1, gmix, w2, wt, bfp, gqp, gq_col, gkp, e2k, cq, tri)


def _attn_kernel(qat_ref, ka_ref, vtb_ref, o_ref, m_sc, l_sc, acc_sc, *, qblk, kblk):
    qi = pl.program_id(1)
    ratio = qblk // kblk
    m_sc[...] = jnp.full_like(m_sc, NEG_BIG)
    l_sc[...] = jnp.zeros_like(l_sc)
    acc_sc[...] = jnp.zeros_like(acc_sc)
    key = lax.broadcasted_iota(jnp.int32, (kblk, qblk), 0)
    qry = lax.broadcasted_iota(jnp.int32, (kblk, qblk), 1)

    def block(kb, diag):
        ks = pl.multiple_of(kb * kblk, kblk)
        for h in range(N_HEADS):
            k = ka_ref[0, pl.ds(ks, kblk), HEAD_PAD * h:HEAD_PAD * (h + 1)]
            st = _dot(k, qat_ref[0, HEAD_PAD * h:HEAD_PAD * (h + 1), :])
            if diag is not None:
                st = jnp.where(key + diag * kblk <= qry, st, NEG_BIG)
            m_old = m_sc[h]
            m_new = jnp.maximum(m_old, jnp.max(st, axis=0, keepdims=True))
            alpha = jnp.exp(m_old - m_new)
            p = jnp.exp(st - m_new)
            l_sc[h] = alpha * l_sc[h] + jnp.sum(p, axis=0, keepdims=True)
            vt = vtb_ref[0, kb, D_HEAD * h:D_HEAD * (h + 1), :]
            rows = slice(D_HEAD * h, D_HEAD * (h + 1))
            acc_sc[rows, :] = alpha * acc_sc[rows, :] + _dot(vt, p.astype(BF16))
            m_sc[h] = m_new

    def body(kb, carry):
        block(kb, None)
        return carry

    lax.fori_loop(0, qi * ratio, body, 0)
    for d in range(ratio):
        block(qi * ratio + d, d)

    for hp in range(N_HEADS // 2):
        inv = jnp.concatenate([jnp.broadcast_to(1.0 / l_sc[2 * hp + i], (D_HEAD, qblk)) for i in range(2)], axis=0)
        pair = acc_sc[LANES * hp:LANES * (hp + 1), :] * inv
        o_ref[0, :, LANES * hp:LANES * (hp + 1)] = pair.T.astype(BF16)


def _attn(qat, ka, vtb, *, qblk, kblk):
    nb, nt, _ = ka.shape
    assert vtb.shape == (nb, nt // kblk, A_WIDTH, kblk) and qblk % kblk == 0
    return pl.pallas_call(
        functools.partial(_attn_kernel, qblk=qblk, kblk=kblk),
        grid=(nb, nt // qblk),
        in_specs=[pl.BlockSpec((1, QA_WIDTH, qblk), lambda b, i: (b, 0, i)),
                  pl.BlockSpec((1, nt, QA_WIDTH), lambda b, i: (b, 0, 0)),
                  pl.BlockSpec((1, nt // kblk, A_WIDTH, kblk), lambda b, i: (b, 0, 0, 0))],
        out_specs=pl.BlockSpec((1, qblk, A_WIDTH), lambda b, i: (b, i, 0)),
        out_shape=jax.ShapeDtypeStruct((nb, nt, A_WIDTH), BF16),
        scratch_shapes=[pltpu.VMEM((N_HEADS, 1, qblk), F32), pltpu.VMEM((N_HEADS, 1, qblk), F32),
                        pltpu.VMEM((A_WIDTH, qblk), F32)],
        compiler_params=_cparams("parallel", "arbitrary"),
        name="attn",
    )(qat, ka, vtb)


def _pool_logf_kernel(lf_ref, m_ref, o_ref):
    hi, mid, lo = _split3(lf_ref[...])
    m = m_ref[...]
    o_ref[...] = _dot(hi, m) + _dot(mid, m) + _dot(lo, m)


def _pool_logf(lf_rows):
    n_rows = lf_rows.shape[0]
    tile = next(c for c in (2048, 1024, 512, 256, 128, 64, 32, 16, 8) if n_rows % c == 0)
    j_src = jnp.arange(PAGE)
    suffix = j_src[:, None] > j_src[None, :]
    m = jnp.concatenate([suffix, jnp.ones((PAGE, PAGE), bool)], axis=1).astype(BF16)
    return pl.pallas_call(
        _pool_logf_kernel,
        grid=(n_rows // tile,),
        in_specs=[pl.BlockSpec((tile, PAGE), lambda i: (i, 0)), _const_spec((PAGE, 2 * PAGE))],
        out_specs=pl.BlockSpec((tile, 2 * PAGE), lambda i: (i, 0)),
        out_shape=jax.ShapeDtypeStruct((n_rows, 2 * PAGE), F32),
        compiler_params=_cparams("parallel"),
        name="pool_logf",
    )(lf_rows, m)


def _sattn_kernel(pt_ref, qbd_ref, qabd_ref, kan_ref, vn_ref, *rest, pp, ts):
    k_refs, v_refs, r_refs = rest[:pp], rest[pp:2 * pp], rest[2 * pp:3 * pp]
    o_ref, kbuf, vbuf, m_sc, l_sc, acc_sc, tot_sc = rest[3 * pp:]
    del pt_ref
    j = pl.program_id(1)
    nrow = N_HEADS * ts

    @pl.when(j == 0)
    def _():
        m_sc[...] = jnp.full_like(m_sc, NEG_BIG)
        l_sc[...] = jnp.zeros_like(l_sc)
        acc_sc[...] = jnp.zeros_like(acc_sc)
        tot_sc[...] = jnp.zeros_like(tot_sc)

    for i in range(pp):
        kbuf[:, PAGE * i:PAGE * (i + 1)] = k_refs[i][0].astype(BF16)
        vbuf[:, PAGE * i:PAGE * (i + 1)] = v_refs[i][0].astype(BF16)

    def update(s, pv):
        m_old = m_sc[...]
        m_new = jnp.maximum(m_old, jnp.max(s, axis=-1, keepdims=True))
        alpha = jnp.exp(m_old - m_new)
        p = jnp.exp(s - m_new)
        l_sc[...] = alpha * l_sc[...] + jnp.sum(p, axis=-1, keepdims=True)
        acc_sc[...] = alpha * acc_sc[...] + pv(p.astype(BF16))
        m_sc[...] = m_new

    later = tot_sc[...]
    biases = []
    for i in range(pp):
        r = r_refs[i][0]
        biases.append(jnp.concatenate([r[:, :PAGE] + later] * ts, axis=0))
        later = later + r[:, PAGE:]
    tot_sc[...] = later
    s = _dot(qbd_ref[0], kbuf[...]) + jnp.concatenate(biases, axis=1)
    update(s, lambda p: _dot_nt(p, vbuf[...]))

    @pl.when(j == pl.num_programs(1) - 1)
    def _():
        s_new = _dot_nt(qabd_ref[0], kan_ref[0])
        r_i = lax.broadcasted_iota(jnp.int32, (nrow, ts), 0)
        c_i = lax.broadcasted_iota(jnp.int32, (nrow, ts), 1)
        update(jnp.where((r_i // N_HEADS) >= c_i, s_new, NEG_BIG), lambda p: _dot(p, vn_ref[0]))
        o = acc_sc[...] / l_sc[...]
        lane_head = lax.broadcasted_iota(jnp.int32, (N_HEADS, A_WIDTH), 1) // D_HEAD
        own = lane_head == lax.broadcasted_iota(jnp.int32, (N_HEADS, A_WIDTH), 0)
        rows = [jnp.sum(jnp.where(own, o[N_HEADS * t:N_HEADS * (t + 1), :], 0.0), axis=0, keepdims=True)
                for t in range(ts)]
        o_ref[0] = jnp.concatenate(rows, axis=0).astype(BF16)


def _sattn(page_table, qbd, qabd, kan, vn, cache_kt, cache_vt, rsuf, *, pp):
    nb, n_pages = page_table.shape
    ts = kan.shape[1]
    nrow = N_HEADS * ts
    steps = n_pages // pp
    pt_flat = page_table.reshape(-1)

    def page_spec(i, shape):
        def imap(b, j, pt):
            return (pt[b * n_pages + (n_pages - 1 - (j * pp + i))], 0, 0)
        return pl.BlockSpec(shape, imap)

    def per_seq(shape):
        return pl.BlockSpec(shape, lambda b, j, pt: (b, 0, 0))

    in_specs = ([per_seq((1, nrow, A_WIDTH)), per_seq((1, nrow, QA_WIDTH)),
                 per_seq((1, ts, QA_WIDTH)), per_seq((1, ts, A_WIDTH))]
                + [page_spec(i, (1, A_WIDTH, PAGE)) for i in range(pp)]
                + [page_spec(i, (1, A_WIDTH, PAGE)) for i in range(pp)]
                + [page_spec(i, (1, N_HEADS, 2 * PAGE)) for i in range(pp)])
    grid_spec = pltpu.PrefetchScalarGridSpec(
        num_scalar_prefetch=1,
        grid=(nb, steps),
        in_specs=in_specs,
        out_specs=per_seq((1, ts, A_WIDTH)),
        scratch_shapes=[pltpu.VMEM((A_WIDTH, pp * PAGE), BF16), pltpu.VMEM((A_WIDTH, pp * PAGE), BF16),
                        pltpu.VMEM((nrow, 1), F32), pltpu.VMEM((nrow, 1), F32),
                        pltpu.VMEM((nrow, A_WIDTH), F32), pltpu.VMEM((N_HEADS, PAGE), F32)],
    )
    return pl.pallas_call(
        functools.partial(_sattn_kernel, pp=pp, ts=ts),
        grid_spec=grid_spec,
        out_shape=jax.ShapeDtypeStruct((nb, ts, A_WIDTH), BF16),
        compiler_params=_cparams("parallel", "arbitrary"),
        name="sattn",
    )(pt_flat, qbd, qabd, kan, vn, *([cache_kt] * pp), *([cache_vt] * pp), *([rsuf] * pp))


def _gelu_tanh(x):
    return 0.5 * x * (1.0 + jnp.tanh(math.sqrt(2.0 / math.pi) * (x + 0.044715 * (x * x * x))))


def _s5_kernel(u_ref, h0_ref, a_ref, bmat_ref, cmat_ref, d_ref, y_ref, ht_ref, hist, state,
               *, tc, bs, rb, lc):
    t = pl.program_id(0)

    @pl.when(t == 0)
    def _():
        state[...] = h0_ref[...]

    u = u_ref[...].reshape(tc * bs, S_WIDTH)
    ub = u.astype(BF16)
    for n in range(2 * N_STATE // MXU_TILE):
        slab = LANES * ((n % (N_STATE // MXU_TILE)) // 2)
        cols = slice(MXU_TILE * n, MXU_TILE * (n + 1))
        hist[:, cols] = _dot(ub[:, slab:slab + LANES], bmat_ref[slab:slab + LANES, cols])

    for c in range(N_STATE // lc):
        re = slice(lc * c, lc * (c + 1))
        im = slice(N_STATE + lc * c, N_STATE + lc * (c + 1))
        a_re = jnp.broadcast_to(a_ref[:, re], (rb, lc))
        a_im = jnp.broadcast_to(a_ref[:, im], (rb, lc))

        def row_block(r, _, re=re, im=im, a_re=a_re, a_im=a_im):
            r0 = pl.multiple_of(r * rb, rb)

            def step(j, carry):
                h_re, h_im = carry
                row = pl.multiple_of(j * bs + r0, rb)
                n_re = a_re * h_re - a_im * h_im + hist[pl.ds(row, rb), re]
                n_im = a_re * h_im + a_im * h_re + hist[pl.ds(row, rb), im]
                hist[pl.ds(row, rb), re] = n_re
                hist[pl.ds(row, rb), im] = n_im
                return n_re, n_im

            h_re, h_im = lax.fori_loop(0, tc, step, (state[pl.ds(r0, rb), re], state[pl.ds(r0, rb), im]))
            state[pl.ds(r0, rb), re] = h_re
            state[pl.ds(r0, rb), im] = h_im
            return 0

        lax.fori_loop(0, bs // rb, row_block, 0)

    half = N_STATE // (S_WIDTH // MXU_TILE)
    ys = []
    for m_ in range(S_WIDTH // MXU_TILE):
        cols = slice(MXU_TILE * m_, MXU_TILE * (m_ + 1))
        re = slice(half * m_, half * (m_ + 1))
        im = slice(N_STATE + half * m_, N_STATE + half * (m_ + 1))
        ys.append(_dot(hist[:, re].astype(BF16), cmat_ref[re, cols])
                  + _dot(hist[:, im].astype(BF16), cmat_ref[im, cols]))
    y = jnp.concatenate(ys, axis=1) + d_ref[...] * u
    y_ref[...] = _gelu_tanh(y).astype(BF16).reshape(y_ref.shape)
    ht_ref[...] = state[...]


def _s5(u_tm, h0, a_row, bmat, cmat, d_row, *, tc):
    nt, bs, _ = u_tm.shape
    rb = 16 if bs % 16 == 0 else 8
    assert bs % rb == 0 and nt % tc == 0
    return pl.pallas_call(
        functools.partial(_s5_kernel, tc=tc, bs=bs, rb=rb, lc=256),
        grid=(nt // tc,),
        in_specs=[pl.BlockSpec((tc, bs, S_WIDTH), lambda t: (t, 0, 0)),
                  _const_spec((bs, 2 * N_STATE)), _const_spec((1, 2 * N_STATE)),
                  _const_spec((S_WIDTH, 2 * N_STATE)), _const_spec((2 * N_STATE, S_WIDTH)),
                  _const_spec((1, S_WIDTH))],
        out_specs=[pl.BlockSpec((tc, bs, S_WIDTH), lambda t: (t, 0, 0)),
                   pl.BlockSpec((bs, 2 * N_STATE), lambda t: (0, 0))],
        out_shape=[jax.ShapeDtypeStruct((nt, bs, S_WIDTH), BF16),
                   jax.ShapeDtypeStruct((bs, 2 * N_STATE), F32)],
        scratch_shapes=[pltpu.VMEM((tc * bs, 2 * N_STATE), F32), pltpu.VMEM((bs, 2 * N_STATE), F32)],
        compiler_params=_cparams("arbitrary"),
        name="s5",
    )(u_tm, h0, a_row, bmat, cmat, d_row)


def _post_kernel(x_ref, a_ref, ys_ref, sga_ref, sgs_ref, g1_ref, sh2_ref, sc2_ref, g2_ref, gffn_ref,
                 wglu_ref, bglu_ref, wupa_ref, wups_ref, wout_ref, wffu_ref, wffd_ref, y_ref,
                 *, bb, tt, ffc):
    rows = bb * tt
    ys = ys_ref[...].reshape(rows, S_WIDTH)
    glu = _dot(ys, wglu_ref[...]) + bglu_ref[...]
    s = (ys.astype(F32) * jax.nn.sigmoid(glu)).astype(BF16)
    a = a_ref[...].reshape(rows, A_WIDTH)
    merged = (sga_ref[...].reshape(rows, D_MODEL).astype(F32) * _dot(a, wupa_ref[...])
              + sgs_ref[...].reshape(rows, D_MODEL).astype(F32) * _dot(s, wups_ref[...]))
    mix = _dot(merged.astype(BF16), wout_ref[...]).reshape(bb, tt, D_MODEL)
    x1 = x_ref[...] + g1_ref[...] * mix

    ms = jnp.mean(x1 * x1, axis=-1, keepdims=True)
    h2 = x1 * lax.rsqrt(ms + NORM_EPS) * gffn_ref[...]
    h2 = (h2 * (1.0 + sc2_ref[...]) + sh2_ref[...]).reshape(rows, D_MODEL).astype(BF16)
    down = jnp.zeros((rows, D_MODEL), F32)
    for c in range(D_FF // ffc):
        up = jnp.maximum(_dot(h2, wffu_ref[:, ffc * c:ffc * (c + 1)]), 0.0)
        down = down + _dot((up * up).astype(BF16), wffd_ref[ffc * c:ffc * (c + 1), :])
    y_ref[...] = x1 + g2_ref[...] * down.reshape(bb, tt, D_MODEL)


def _post(x, a, ys, sga, sgs, g1, sh2, sc2, g2, gffn, wglu, bglu, wupa, wups, wout, wffu, wffd,
          *, bb, tt, ys_time_major):
    nb, nt_total, _ = x.shape

    def tok(width):
        return pl.BlockSpec((bb, tt, width), lambda b, t: (b, t, 0))

    def mod():
        return pl.BlockSpec((bb, 1, D_MODEL), lambda b, t: (b, 0, 0))

    if ys_time_major:
        assert bb == 1
        ys_spec = pl.BlockSpec((tt, S_WIDTH), lambda b, t: (t, b))
    else:
        ys_spec = tok(S_WIDTH)

    return pl.pallas_call(
        functools.partial(_post_kernel, bb=bb, tt=tt, ffc=1024),
        grid=(nb // bb, nt_total // tt),
        in_specs=[tok(D_MODEL), tok(A_WIDTH), ys_spec, tok(D_MODEL), tok(D_MODEL),
                  mod(), mod(), mod(), mod(), _const_spec((1, D_MODEL)),
                  _const_spec((S_WIDTH, S_WIDTH)), _const_spec((1, S_WIDTH)),
                  _const_spec((A_WIDTH, D_MODEL)), _const_spec((S_WIDTH, D_MODEL)),
                  _const_spec((D_MODEL, D_MODEL)), _const_spec((D_MODEL, D_FF)), _const_spec((D_FF, D_MODEL))],
        out_specs=tok(D_MODEL),
        out_shape=jax.ShapeDtypeStruct((nb, nt_total, D_MODEL), F32),
        compiler_params=_cparams("parallel", "parallel"),
        name="post",
    )(x, a, ys, sga, sgs, g1, sh2, sc2, g2, gffn, wglu, bglu, wupa, wups, wout, wffu, wffd)


def _block_diag_heads(q, width):
    nb, ts, _ = q.shape
    q5 = q.reshape(nb, ts, 1, N_HEADS, width)
    keep = jnp.eye(N_HEADS, dtype=bool).reshape(1, 1, N_HEADS, N_HEADS, 1)
    return jnp.where(keep, q5, jnp.zeros((), q.dtype)).reshape(nb, ts * N_HEADS, N_HEADS * width)


def kernel(x_prompt, x_sample, cache_k, cache_v, cache_logf, state_ssm_re, state_ssm_im, page_table,
           c_prompt, c_sample, w_ada, b_ada, norm_mix_g, norm_ffn_g, w_in, b_fgate, q_norm_g, k_norm_g,
           ssm_lambda_re, ssm_lambda_im, ssm_log_dt, ssm_b_re, ssm_b_im, ssm_c_re, ssm_c_im, ssm_d,
           w_glu, b_glu, w_up_a, w_up_s, w_out, w_ffn_up, w_ffn_down):
    nbp, ntp, _ = x_prompt.shape
    nbs, nts, _ = x_sample.shape
    n_pool = cache_k.shape[0]

    a3 = 3 * A_WIDTH
    wf_pad = jnp.pad(w_in[:, a3:a3 + N_HEADS], ((0, 0), (0, LANES - N_HEADS)))
    w2 = jnp.concatenate([w_in[:, :a3], w_in[:, a3 + N_HEADS:], wf_pad], axis=1).astype(BF16)
    bfp = jnp.pad(b_fgate, (0, LANES - N_HEADS)).reshape(1, LANES)
    wt = jnp.concatenate([w_in[:, :A_WIDTH].T, w_in[:, 2 * A_WIDTH:a3].T], axis=0).astype(BF16)
    gqp = jnp.pad(q_norm_g, (0, LANES - D_HEAD)).reshape(1, LANES)
    gq_col = q_norm_g.reshape(D_HEAD, 1)
    gkp = jnp.pad(k_norm_g, (0, LANES - D_HEAD)).reshape(1, LANES)
    heads = jnp.arange(N_HEADS)
    e2k = jnp.zeros((LANES, QA_WIDTH), F32)
    cq = jnp.zeros((1, QA_WIDTH), F32)
    for piece in range(N_SPLIT):
        e2k = e2k.at[piece * N_HEADS + heads, HEAD_PAD * heads + D_HEAD + piece].set(-1.0)
        cq = cq.at[0, HEAD_PAD * heads + D_HEAD + piece].set(1.0)
    e2k = e2k.astype(BF16)
    gmix = norm_mix_g.reshape(1, D_MODEL)
    gffn = norm_ffn_g.reshape(1, D_MODEL)
    post_w = (w_glu.astype(BF16), b_glu.reshape(1, S_WIDTH), w_up_a.astype(BF16), w_up_s.astype(BF16),
              w_out.astype(BF16), w_ffn_up.astype(BF16), w_ffn_down.astype(BF16))

    mod = _adaln(jnp.concatenate([c_prompt, c_sample], axis=0), w_ada.astype(BF16), b_ada)
    mod = mod.reshape(nbp + nbs, 1, 6, D_MODEL)
    mods_p = [mod[:nbp, :, i, :] for i in range(6)]
    mods_s = [mod[nbp:, :, i, :] for i in range(6)]

    ab_re, ab_im, bb_re, bb_im = _s5_disc(ssm_lambda_re, ssm_lambda_im, ssm_log_dt, ssm_b_re, ssm_b_im)
    a_row = jnp.concatenate([ab_re.reshape(1, N_STATE), ab_im.reshape(1, N_STATE)], axis=1)
    eye_g = jnp.eye(N_GROUPS, dtype=F32)

    def b_blockdiag(bb):
        return jnp.einsum('gcp,gh->gchp', bb, eye_g).reshape(S_WIDTH, N_STATE)

    def c_blockdiag(cc):
        return jnp.einsum('gcp,gh->gphc', cc, eye_g).reshape(N_STATE, S_WIDTH)

    bmat = jnp.concatenate([b_blockdiag(bb_re), b_blockdiag(bb_im)], axis=1).astype(BF16)
    cmat = jnp.concatenate([c_blockdiag(ssm_c_re), -c_blockdiag(ssm_c_im)], axis=0).astype(BF16)
    d_row = ssm_d.reshape(1, S_WIDTH)

    tt = min(512, ntp)
    ablk = min(512, ntp)
    qat, ka, k_p, vt_p, vtb, lf_p, u_tm, sga, sgs = _inproj(
        x_prompt, mods_p[0], mods_p[1], gmix, w2, wt, bfp, gqp, gq_col, gkp, e2k, cq,
        bb=1, tt=tt, prompt=True, kblk=ablk)
    attn_p = _attn(qat, ka, vtb, qblk=min(512, ntp), kblk=ablk)
    v_p = jnp.transpose(vt_p.reshape(nbp, N_HEADS, D_HEAD, ntp), (0, 3, 1, 2))
    h0 = jnp.zeros((nbp, 2 * N_STATE), F32)
    ys_tm, ht_p = _s5(u_tm.reshape(ntp, nbp, S_WIDTH), h0, a_row, bmat, cmat, d_row, tc=min(32, ntp))
    y_prompt = _post(x_prompt, attn_p, ys_tm.reshape(ntp, nbp * S_WIDTH), sga, sgs,
                     mods_p[2], mods_p[3], mods_p[4], mods_p[5], gffn, *post_w,
                     bb=1, tt=tt, ys_time_major=True)

    bbs = min(64, nbs)
    qa_s, ka_s, k_s, v_s, vb_s, lf_s, u_s, sga_s, sgs_s, qn_s = _inproj(
        x_sample, mods_s[0], mods_s[1], gmix, w2, wt, bfp, gqp, gq_col, gkp, e2k, cq,
        bb=bbs, tt=nts, prompt=False, kblk=ablk)
    lf_rows = jnp.swapaxes(cache_logf, 1, 2).reshape(n_pool * N_HEADS, PAGE)
    rsuf = _pool_logf(lf_rows).reshape(n_pool, N_HEADS, 2 * PAGE)
    cache_kt = jnp.transpose(cache_k, (0, 2, 3, 1)).reshape(n_pool, A_WIDTH, PAGE)
    cache_vt = jnp.transpose(cache_v, (0, 2, 3, 1)).reshape(n_pool, A_WIDTH, PAGE)
    attn_s = _sattn(page_table, _block_diag_heads(qn_s, D_HEAD), _block_diag_heads(qa_s, HEAD_PAD),
                    ka_s, vb_s, cache_kt, cache_vt, rsuf, pp=min(16, page_table.shape[1]))
    h0_s = jnp.concatenate([state_ssm_re.reshape(nbs, N_STATE), state_ssm_im.reshape(nbs, N_STATE)], axis=1)
    ys_s_tm, ht_s = _s5(jnp.swapaxes(u_s, 0, 1), h0_s, a_row, bmat, cmat, d_row, tc=nts)
    y_sample = _post(x_sample, attn_s, jnp.swapaxes(ys_s_tm, 0, 1), sga_s, sgs_s,
                     mods_s[2], mods_s[3], mods_s[4], mods_s[5], gffn, *post_w,
                     bb=bbs, tt=nts, ys_time_major=False)

    def heads4(z):
        return z.reshape(z.shape[0], z.shape[1], N_HEADS, D_HEAD)

    def state3(z):
        return z.reshape(z.shape[0], N_GROUPS, STATE_P)

    return (y_prompt, y_sample, heads4(k_p), heads4(v_p), lf_p,
            state3(ht_p[:, :N_STATE]), state3(ht_p[:, N_STATE:]),
            heads4(k_s), heads4(v_s), lf_s,
            state3(ht_s[:, :N_STATE]), state3(ht_s[:, N_STATE:]))
```

```python
import functools
import math

import jax
import jax.numpy as jnp
from jax import lax
from jax.experimental import pallas as pl
from jax.experimental.pallas import tpu as pltpu

F32 = jnp.float32
BF16 = jnp.bfloat16

D_MODEL = 1024
N_HEADS = 8
D_HEAD = 64
A_WIDTH = N_HEADS * D_HEAD
S_WIDTH = D_MODEL // 2
GROUP_CH = 16
N_GROUPS = S_WIDTH // GROUP_CH
STATE_P = 64
N_STATE = N_GROUPS * STATE_P
D_FF = 4 * D_MODEL
PAGE = 128
ATTN_SCALE = 1.0 / math.sqrt(D_HEAD)
LOG2_E = math.log2(math.e)
NORM_EPS = 1e-6
NEG_BIG = -1e30

LANES = 128
MXU_TILE = 256
HEAD_PAD = 128
QA_WIDTH = N_HEADS * HEAD_PAD
N_SPLIT = 3
VMEM_LIMIT = 56 * 1024 * 1024

_OQ, _OK, _OV, _OU, _OGA, _OGS, _OF, _W2 = 0, 512, 1024, 1536, 2048, 3072, 4096, 4224


def _cparams(*sem):
    return pltpu.CompilerParams(dimension_semantics=sem, vmem_limit_bytes=VMEM_LIMIT)


def _const_spec(shape):
    nd = len(shape)
    return pl.BlockSpec(shape, lambda *_: (0,) * nd, pipeline_mode=pl.Buffered(1))


def _split3(x):
    hi = x.astype(BF16)
    r1 = x - hi.astype(F32)
    mid = r1.astype(BF16)
    lo = (r1 - mid.astype(F32)).astype(BF16)
    return hi, mid, lo


def _dot(a, b):
    return jnp.dot(a, b, preferred_element_type=F32)


def _dot_nt(a, b):
    return lax.dot_general(a, b, (((1,), (1,)), ((), ())), preferred_element_type=F32)


def _adaln_kernel(c_ref, w_ref, b_ref, o_ref):
    c = c_ref[...]
    s = (c * jax.nn.sigmoid(c)).astype(BF16)
    o_ref[...] = _dot(s, w_ref[...]) + b_ref[...]


def _adaln(c_all, w_ada_bf, b_ada):
    n = c_all.shape[0]
    tn = 1024
    return pl.pallas_call(
        _adaln_kernel,
        grid=(6 * D_MODEL // tn,),
        in_specs=[pl.BlockSpec((n, D_MODEL), lambda j: (0, 0)),
                  pl.BlockSpec((D_MODEL, tn), lambda j: (0, j)),
                  pl.BlockSpec((1, tn), lambda j: (0, j))],
        out_specs=pl.BlockSpec((n, tn), lambda j: (0, j)),
        out_shape=jax.ShapeDtypeStruct((n, 6 * D_MODEL), F32),
        compiler_params=_cparams("parallel"),
        name="adaln",
    )(c_all, w_ada_bf, b_ada.reshape(1, -1))


def _s5_disc_kernel(lr_ref, li_ref, ldt_ref, br_ref, bi_ref, abr_ref, abi_ref, bbr_ref, bbi_ref):
    lr = lr_ref[...]
    li = li_ref[...]
    dt = jnp.exp(ldt_ref[...])
    mag = jnp.exp(lr * dt)
    ab_re = mag * jnp.cos(li * dt)
    ab_im = mag * jnp.sin(li * dt)
    nr, ni = ab_re - 1.0, ab_im
    den = lr * lr + li * li
    f_re = (nr * lr + ni * li) / den
    f_im = (ni * lr - nr * li) / den
    br = br_ref[...]
    bi = bi_ref[...]
    abr_ref[...] = ab_re
    abi_ref[...] = ab_im
    bbr_ref[...] = f_re * br - f_im * bi
    bbi_ref[...] = f_re * bi + f_im * br


def _s5_disc(lam_re, lam_im, log_dt, b_re, b_im):
    g3 = jax.ShapeDtypeStruct((N_GROUPS, 1, STATE_P), F32)
    b3 = jax.ShapeDtypeStruct((N_GROUPS, GROUP_CH, STATE_P), F32)
    return pl.pallas_call(
        _s5_disc_kernel,
        out_shape=(g3, g3, b3, b3),
        name="s5_disc",
    )(lam_re.reshape(N_GROUPS, 1, STATE_P), lam_im.reshape(N_GROUPS, 1, STATE_P),
      log_dt.reshape(N_GROUPS, 1, 1),
      jnp.swapaxes(b_re, 1, 2), jnp.swapaxes(b_im, 1, 2))


def _inproj_kernel(x_ref, sh_ref, sc_ref, gmix_ref, w_ref, wt_ref, bf_ref, gq_ref, gqcol_ref, gk_ref,
                   e2k_ref, cq_ref, tri_ref, *rest, bb, tt, prompt, kblk):
    if prompt:
        qat_ref, ka_ref, k_ref, vt_ref, vtb_ref, lf_ref, u_ref, sga_ref, sgs_ref, carry_ref = rest
    else:
        qa_ref, ka_ref, k_ref, v_ref, vb_ref, lf_ref, u_ref, sga_ref, sgs_ref, qn_ref, carry_ref = rest
    rows = bb * tt
    t = pl.program_id(1)

    x = x_ref[...]
    ms = jnp.mean(x * x, axis=-1, keepdims=True)
    h = x * lax.rsqrt(ms + NORM_EPS) * gmix_ref[...]
    h = h * (1.0 + sc_ref[...]) + sh_ref[...]
    hb = h.reshape(rows, D_MODEL).astype(BF16)

    lane = lax.broadcasted_iota(jnp.int32, (1, LANES), 1)
    low_half = lane < D_HEAD

    zf = _dot(hb, w_ref[:, _OF:_W2]) + bf_ref[...]
    lf = jnp.minimum(zf, 0.0) - jnp.log1p(jnp.exp(-jnp.abs(zf)))
    lf = jnp.where(lane < N_HEADS, lf, 0.0)
    lf_ref[...] = lf[:, :N_HEADS].reshape(lf_ref.shape)
    hi, mid, lo = _split3(lf)
    packed = (hi.astype(F32) + pltpu.roll(mid.astype(F32), N_HEADS, 1)
              + pltpu.roll(lo.astype(F32), 2 * N_HEADS, 1)).astype(BF16)
    gc = _dot(tri_ref[...], packed)
    g = gc + pltpu.roll(gc, LANES - N_HEADS, 1) + pltpu.roll(gc, LANES - 2 * N_HEADS, 1)
    g = jnp.where(lane < N_HEADS, g, 0.0)
    if bb == 1:
        @pl.when(t == 0)
        def _():
            carry_ref[...] = jnp.zeros_like(carry_ref)
        g = g + carry_ref[...]
        carry_ref[...] = g[rows - 1:rows, :]
    logit_scale = LOG2_E if prompt else 1.0
    ghi, gmid, glo = _split3(g * logit_scale)
    gpacked = (ghi.astype(F32) + pltpu.roll(gmid.astype(F32), N_HEADS, 1)
               + pltpu.roll(glo.astype(F32), 2 * N_HEADS, 1)).astype(BF16)
    kaug = _dot(gpacked, e2k_ref[...])

    def widen_norm(z, gain):
        out = []
        for part in (z, pltpu.roll(z, D_HEAD, 1)):
            e = jnp.where(low_half, part, 0.0)
            ss = jnp.sum(e * e, axis=-1, keepdims=True)
            out.append(e * lax.rsqrt(ss * (1.0 / D_HEAD) + NORM_EPS) * gain)
        return out

    gq = gq_ref[...] * ATTN_SCALE
    gk = gk_ref[...]
    for j in range(N_HEADS // 2):
        zk = _dot(hb, w_ref[:, _OK + LANES * j:_OK + LANES * (j + 1)])
        k_even, k_odd = widen_norm(zk, gk)
        for i, kh in enumerate((k_even, k_odd)):
            c0 = HEAD_PAD * (2 * j + i)
            ka_ref[..., c0:c0 + HEAD_PAD] = (kh + kaug[:, c0:c0 + HEAD_PAD]).astype(BF16).reshape(bb, tt, HEAD_PAD)
        k_ref[..., LANES * j:LANES * (j + 1)] = (k_even + pltpu.roll(k_odd, D_HEAD, 1)).reshape(bb, tt, LANES)
        if not prompt:
            zq = _dot(hb, w_ref[:, _OQ + LANES * j:_OQ + LANES * (j + 1)])
            q_even, q_odd = widen_norm(zq, gq)
            for i, qh in enumerate((q_even, q_odd)):
                c0 = HEAD_PAD * (2 * j + i)
                qa_ref[..., c0:c0 + HEAD_PAD] = (
                    (qh + cq_ref[:, c0:c0 + HEAD_PAD]).astype(BF16).reshape(bb, tt, HEAD_PAD))
            qn_ref[..., LANES * j:LANES * (j + 1)] = (
                (q_even + pltpu.roll(q_odd, D_HEAD, 1)).astype(BF16).reshape(bb, tt, LANES))

    if prompt:
        zqt = _dot_nt(wt_ref[:A_WIDTH, :], hb)
        gq_col = gqcol_ref[...] * (ATTN_SCALE * logit_scale)
        sub = lax.broadcasted_iota(jnp.int32, (D_HEAD, tt), 0)
        ones_rows = jnp.where(sub < N_SPLIT, 1.0, 0.0).astype(BF16)
        for hd in range(N_HEADS):
            z = zqt[D_HEAD * hd:D_HEAD * (hd + 1), :]
            ss = jnp.sum(z * z, axis=0, keepdims=True)
            qn_t = z * lax.rsqrt(ss * (1.0 / D_HEAD) + NORM_EPS) * gq_col
            qat_ref[0, HEAD_PAD * hd:HEAD_PAD * hd + D_HEAD, :] = qn_t.astype(BF16)
            qat_ref[0, HEAD_PAD * hd + D_HEAD:HEAD_PAD * (hd + 1), :] = ones_rows
        zvt = _dot_nt(wt_ref[A_WIDTH:, :], hb)
        vt_ref[0] = zvt
        for c in range(tt // kblk):
            vtb_ref[0, c] = zvt[:, kblk * c:kblk * (c + 1)].astype(BF16)
    else:
        zv = _dot(hb, w_ref[:, _OV:_OU])
        v_ref[...] = zv.reshape(v_ref.shape)
        vb_ref[...] = zv.astype(BF16).reshape(vb_ref.shape)
    u_ref[...] = _dot(hb, w_ref[:, _OU:_OGA]).reshape(u_ref.shape)
    sga_ref[...] = jax.nn.sigmoid(_dot(hb, w_ref[:, _OGA:_OGS])).astype(BF16).reshape(sga_ref.shape)
    sgs_ref[...] = jax.nn.sigmoid(_dot(hb, w_ref[:, _OGS:_OF])).astype(BF16).reshape(sgs_ref.shape)


def _inproj(x, sh1, sc1, gmix, w2, wt, bfp, gqp, gq_col, gkp, e2k, cq, *, bb, tt, prompt, kblk):
    nb, nt_total, _ = x.shape
    u_time_major = prompt
    rows = bb * tt
    idx = jnp.arange(rows)
    tri = ((idx[:, None] >= idx[None, :]) & (idx[:, None] // tt == idx[None, :] // tt)).astype(BF16)
    grid = (nb // bb, nt_total // tt)
    assert bb == 1 or grid[1] == 1

    def tok(width):
        return pl.BlockSpec((bb, tt, width), lambda b, t: (b, t, 0))

    def mod():
        return pl.BlockSpec((bb, 1, D_MODEL), lambda b, t: (b, 0, 0))

    if u_time_major:
        assert bb == 1
        u_shape = jax.ShapeDtypeStruct((nt_total, nb * S_WIDTH), F32)
        u_spec = pl.BlockSpec((tt, S_WIDTH), lambda b, t: (t, b))
    else:
        u_shape = jax.ShapeDtypeStruct((nb, nt_total, S_WIDTH), F32)
        u_spec = tok(S_WIDTH)

    def sds(width, dt):
        return jax.ShapeDtypeStruct((nb, nt_total, width), dt)

    tail_shape = [sds(N_HEADS, F32), u_shape, sds(D_MODEL, BF16), sds(D_MODEL, BF16)]
    tail_specs = [tok(N_HEADS), u_spec, tok(D_MODEL), tok(D_MODEL)]
    if prompt:
        assert tt % kblk == 0
        out_shape = [jax.ShapeDtypeStruct((nb, QA_WIDTH, nt_total), BF16), sds(QA_WIDTH, BF16), sds(A_WIDTH, F32),
                     jax.ShapeDtypeStruct((nb, A_WIDTH, nt_total), F32),
                     jax.ShapeDtypeStruct((nb, nt_total // kblk, A_WIDTH, kblk), BF16)] + tail_shape
        out_specs = [pl.BlockSpec((1, QA_WIDTH, tt), lambda b, t: (b, 0, t)), tok(QA_WIDTH), tok(A_WIDTH),
                     pl.BlockSpec((1, A_WIDTH, tt), lambda b, t: (b, 0, t)),
                     pl.BlockSpec((1, tt // kblk, A_WIDTH, kblk), lambda b, t: (b, t, 0, 0))] + tail_specs
    else:
        out_shape = ([sds(QA_WIDTH, BF16), sds(QA_WIDTH, BF16), sds(A_WIDTH, F32), sds(A_WIDTH, F32),
                      sds(A_WIDTH, BF16)] + tail_shape + [sds(A_WIDTH, BF16)])
        out_specs = ([tok(QA_WIDTH), tok(QA_WIDTH), tok(A_WIDTH), tok(A_WIDTH), tok(A_WIDTH)]
                     + tail_specs + [tok(A_WIDTH)])

    return pl.pallas_call(
        functools.partial(_inproj_kernel, bb=bb, tt=tt, prompt=prompt, kblk=kblk),
        grid=grid,
        in_specs=[tok(D_MODEL), mod(), mod(), _const_spec((1, D_MODEL)), _const_spec((D_MODEL, _W2)),
                  _const_spec((2 * A_WIDTH, D_MODEL)),
                  _const_spec((1, LANES)), _const_spec((1, LANES)), _const_spec((D_HEAD, 1)), _const_spec((1, LANES)),
                  _const_spec((LANES, QA_WIDTH)), _const_spec((1, QA_WIDTH)), _const_spec((rows, rows))],
        out_specs=out_specs,
        out_shape=out_shape,
        scratch_shapes=[pltpu.VMEM((1, LANES), F32)],
        compiler_params=_cparams("parallel", "arbitrary"),
        name="inproj",
    )(x, sh1, sc1, gmix, w2, wt, bfp, gqp, gq_col, gkp, e2k, cq, tri)


def _attn_kernel(qat_ref, ka_ref, vtb_ref, o_ref, m_sc, l_sc, acc_sc, *, qblk, kblk):
    qi = pl.program_id(1)
    ratio = qblk // kblk
    m_sc[...] = jnp.full_like(m_sc, NEG_BIG)
    l_sc[...] = jnp.zeros_like(l_sc)
    acc_sc[...] = jnp.zeros_like(acc_sc)
    key = lax.broadcasted_iota(jnp.int32, (kblk, qblk), 0)
    qry = lax.broadcasted_iota(jnp.int32, (kblk, qblk), 1)

    def block(kb, diag):
        ks = pl.multiple_of(kb * kblk, kblk)
        for h in range(N_HEADS):
            k = ka_ref[0, pl.ds(ks, kblk), HEAD_PAD * h:HEAD_PAD * (h + 1)]
            st = _dot(k, qat_ref[0, HEAD_PAD * h:HEAD_PAD * (h + 1), :])
            if diag is not None:
                st = jnp.where(key + diag * kblk <= qry, st, NEG_BIG)
            m_old = m_sc[h]
            m_new = jnp.maximum(m_old, jnp.max(st, axis=0, keepdims=True))
            alpha = jnp.exp2(m_old - m_new)
            p = jnp.exp2(st - m_new)
            l_sc[h] = alpha * l_sc[h] + jnp.sum(p, axis=0, keepdims=True)
            vt = vtb_ref[0, kb, D_HEAD * h:D_HEAD * (h + 1), :]
            rows = slice(D_HEAD * h, D_HEAD * (h + 1))
            acc_sc[rows, :] = alpha * acc_sc[rows, :] + _dot(vt, p.astype(BF16))
            m_sc[h] = m_new

    def body(kb, carry):
        block(kb, None)
        return carry

    lax.fori_loop(0, qi * ratio, body, 0)
    for d in range(ratio):
        block(qi * ratio + d, d)

    for hp in range(N_HEADS // 2):
        inv = jnp.concatenate([jnp.broadcast_to(1.0 / l_sc[2 * hp + i], (D_HEAD, qblk)) for i in range(2)], axis=0)
        pair = acc_sc[LANES * hp:LANES * (hp + 1), :] * inv
        o_ref[0, :, LANES * hp:LANES * (hp + 1)] = pair.T.astype(BF16)


def _attn(qat, ka, vtb, *, qblk, kblk):
    nb, nt, _ = ka.shape
    assert vtb.shape == (nb, nt // kblk, A_WIDTH, kblk) and qblk % kblk == 0
    return pl.pallas_call(
        functools.partial(_attn_kernel, qblk=qblk, kblk=kblk),
        grid=(nb, nt // qblk),
        in_specs=[pl.BlockSpec((1, QA_WIDTH, qblk), lambda b, i: (b, 0, i)),
                  pl.BlockSpec((1, nt, QA_WIDTH), lambda b, i: (b, 0, 0)),
                  pl.BlockSpec((1, nt // kblk, A_WIDTH, kblk), lambda b, i: (b, 0, 0, 0))],
        out_specs=pl.BlockSpec((1, qblk, A_WIDTH), lambda b, i: (b, i, 0)),
        out_shape=jax.ShapeDtypeStruct((nb, nt, A_WIDTH), BF16),
        scratch_shapes=[pltpu.VMEM((N_HEADS, 1, qblk), F32), pltpu.VMEM((N_HEADS, 1, qblk), F32),
                        pltpu.VMEM((A_WIDTH, qblk), F32)],
        compiler_params=_cparams("parallel", "arbitrary"),
        name="attn",
    )(qat, ka, vtb)


def _pool_logf_kernel(lf_ref, m_ref, o_ref):
    hi, mid, lo = _split3(lf_ref[...])
    m = m_ref[...]
    o_ref[...] = _dot(hi, m) + _dot(mid, m) + _dot(lo, m)


def _pool_logf(lf_rows):
    n_rows = lf_rows.shape[0]
    tile = next(c for c in (2048, 1024, 512, 256, 128, 64, 32, 16, 8) if n_rows % c == 0)
    j_src = jnp.arange(PAGE)
    suffix = j_src[:, None] > j_src[None, :]
    m = jnp.concatenate([suffix, jnp.ones((PAGE, PAGE), bool)], axis=1).astype(BF16)
    return pl.pallas_call(
        _pool_logf_kernel,
        grid=(n_rows // tile,),
        in_specs=[pl.BlockSpec((tile, PAGE), lambda i: (i, 0)), _const_spec((PAGE, 2 * PAGE))],
        out_specs=pl.BlockSpec((tile, 2 * PAGE), lambda i: (i, 0)),
        out_shape=jax.ShapeDtypeStruct((n_rows, 2 * PAGE), F32),
        compiler_params=_cparams("parallel"),
        name="pool_logf",
    )(lf_rows, m)


def _sattn_kernel(pt_ref, qbd_ref, qabd_ref, kan_ref, vn_ref, *rest, pp, ts, n_pages):
    (k_hbm, v_hbm, r_hbm, o_ref, kpage, vpage, rpage, sem, kbuf, vbuf,
     m_sc, l_sc, acc_sc, tot_sc) = rest
    j = pl.program_id(1)
    n_chunks = pl.num_programs(1)
    step = pl.program_id(0) * n_chunks + j
    slot = step % 2
    nrow = N_HEADS * ts

    def page_copies(step_, slot_):
        seq, chunk = step_ // n_chunks, step_ % n_chunks
        out = []
        for i in range(pp):
            pid = pt_ref[seq * n_pages + (n_pages - 1 - (chunk * pp + i))]
            out.append(pltpu.make_async_copy(k_hbm.at[pid], kpage.at[slot_, i], sem.at[0, slot_]))
            out.append(pltpu.make_async_copy(v_hbm.at[pid], vpage.at[slot_, i], sem.at[1, slot_]))
            out.append(pltpu.make_async_copy(r_hbm.at[pid], rpage.at[slot_, i], sem.at[2, slot_]))
        return out

    @pl.when(step == 0)
    def _():
        for cp in page_copies(step, slot):
            cp.start()

    @pl.when(step + 1 < pl.num_programs(0) * n_chunks)
    def _():
        for cp in page_copies(step + 1, 1 - slot):
            cp.start()

    for cp in page_copies(step, slot):
        cp.wait()

    @pl.when(j == 0)
    def _():
        m_sc[...] = jnp.full_like(m_sc, NEG_BIG)
        l_sc[...] = jnp.zeros_like(l_sc)
        acc_sc[...] = jnp.zeros_like(acc_sc)
        tot_sc[...] = jnp.zeros_like(tot_sc)

    for i in range(pp):
        kbuf[:, PAGE * i:PAGE * (i + 1)] = kpage[slot, i].astype(BF16)
        vbuf[:, PAGE * i:PAGE * (i + 1)] = vpage[slot, i].astype(BF16)

    def update(s, pv):
        m_old = m_sc[...]
        m_new = jnp.maximum(m_old, jnp.max(s, axis=-1, keepdims=True))
        alpha = jnp.exp(m_old - m_new)
        p = jnp.exp(s - m_new)
        l_sc[...] = alpha * l_sc[...] + jnp.sum(p, axis=-1, keepdims=True)
        acc_sc[...] = alpha * acc_sc[...] + pv(p.astype(BF16))
        m_sc[...] = m_new

    later = tot_sc[...]
    biases = []
    for i in range(pp):
        r = rpage[slot, i]
        biases.append(jnp.concatenate([r[:, :PAGE] + later] * ts, axis=0))
        later = later + r[:, PAGE:]
    tot_sc[...] = later
    s = _dot(qbd_ref[0], kbuf[...]) + jnp.concatenate(biases, axis=1)
    update(s, lambda p: _dot_nt(p, vbuf[...]))

    @pl.when(j == pl.num_programs(1) - 1)
    def _():
        s_new = _dot_nt(qabd_ref[0], kan_ref[0])
        r_i = lax.broadcasted_iota(jnp.int32, (nrow, ts), 0)
        c_i = lax.broadcasted_iota(jnp.int32, (nrow, ts), 1)
        update(jnp.where((r_i // N_HEADS) >= c_i, s_new, NEG_BIG), lambda p: _dot(p, vn_ref[0]))
        o = acc_sc[...] / l_sc[...]
        lane_head = lax.broadcasted_iota(jnp.int32, (N_HEADS, A_WIDTH), 1) // D_HEAD
        own = lane_head == lax.broadcasted_iota(jnp.int32, (N_HEADS, A_WIDTH), 0)
        rows = [jnp.sum(jnp.where(own, o[N_HEADS * t:N_HEADS * (t + 1), :], 0.0), axis=0, keepdims=True)
                for t in range(ts)]
        o_ref[0] = jnp.concatenate(rows, axis=0).astype(BF16)


def _sattn(page_table, qbd, qabd, kan, vn, cache_kt, cache_vt, rsuf, *, pp):
    nb, n_pages = page_table.shape
    ts = kan.shape[1]
    nrow = N_HEADS * ts
    steps = n_pages // pp
    pt_flat = page_table.reshape(-1)

    def per_seq(shape):
        return pl.BlockSpec(shape, lambda b, j, pt: (b, 0, 0))

    hbm = pl.BlockSpec(memory_space=pl.ANY)
    grid_spec = pltpu.PrefetchScalarGridSpec(
        num_scalar_prefetch=1,
        grid=(nb, steps),
        in_specs=[per_seq((1, nrow, A_WIDTH)), per_seq((1, nrow, QA_WIDTH)),
                  per_seq((1, ts, QA_WIDTH)), per_seq((1, ts, A_WIDTH)), hbm, hbm, hbm],
        out_specs=per_seq((1, ts, A_WIDTH)),
        scratch_shapes=[pltpu.VMEM((2, pp, A_WIDTH, PAGE), F32), pltpu.VMEM((2, pp, A_WIDTH, PAGE), F32),
                        pltpu.VMEM((2, pp, N_HEADS, 2 * PAGE), F32), pltpu.SemaphoreType.DMA((3, 2)),
                        pltpu.VMEM((A_WIDTH, pp * PAGE), BF16), pltpu.VMEM((A_WIDTH, pp * PAGE), BF16),
                        pltpu.VMEM((nrow, 1), F32), pltpu.VMEM((nrow, 1), F32),
                        pltpu.VMEM((nrow, A_WIDTH), F32), pltpu.VMEM((N_HEADS, PAGE), F32)],
    )
    return pl.pallas_call(
        functools.partial(_sattn_kernel, pp=pp, ts=ts, n_pages=n_pages),
        grid_spec=grid_spec,
        out_shape=jax.ShapeDtypeStruct((nb, ts, A_WIDTH), BF16),
        compiler_params=_cparams("arbitrary", "arbitrary"),
        name="sattn",
    )(pt_flat, qbd, qabd, kan, vn, cache_kt, cache_vt, rsuf)


def _gelu_tanh(x):
    return 0.5 * x * (1.0 + jnp.tanh(math.sqrt(2.0 / math.pi) * (x + 0.044715 * (x * x * x))))


def _s5_kernel(u_ref, h0_ref, a_ref, bmat_ref, cmat_ref, d_ref, y_ref, ht_ref, hist, state,
               *, tc, bs, rb, lc):
    t = pl.program_id(0)

    @pl.when(t == 0)
    def _():
        state[...] = h0_ref[...]

    u = u_ref[...].reshape(tc * bs, S_WIDTH)
    ub = u.astype(BF16)
    for n in range(2 * N_STATE // MXU_TILE):
        slab = LANES * ((n % (N_STATE // MXU_TILE)) // 2)
        cols = slice(MXU_TILE * n, MXU_TILE * (n + 1))
        hist[:, cols] = _dot(ub[:, slab:slab + LANES], bmat_ref[slab:slab + LANES, cols])

    for c in range(N_STATE // lc):
        re = slice(lc * c, lc * (c + 1))
        im = slice(N_STATE + lc * c, N_STATE + lc * (c + 1))
        a_re = jnp.broadcast_to(a_ref[:, re], (rb, lc))
        a_im = jnp.broadcast_to(a_ref[:, im], (rb, lc))

        def row_block(r, _, re=re, im=im, a_re=a_re, a_im=a_im):
            r0 = pl.multiple_of(r * rb, rb)

            def step(j, carry):
                h_re, h_im = carry
                row = pl.multiple_of(j * bs + r0, rb)
                n_re = a_re * h_re - a_im * h_im + hist[pl.ds(row, rb), re]
                n_im = a_re * h_im + a_im * h_re + hist[pl.ds(row, rb), im]
                hist[pl.ds(row, rb), re] = n_re
                hist[pl.ds(row, rb), im] = n_im
                return n_re, n_im

            h_re, h_im = lax.fori_loop(0, tc, step, (state[pl.ds(r0, rb), re], state[pl.ds(r0, rb), im]))
            state[pl.ds(r0, rb), re] = h_re
            state[pl.ds(r0, rb), im] = h_im
            return 0

        lax.fori_loop(0, bs // rb, row_block, 0)

    half = N_STATE // (S_WIDTH // MXU_TILE)
    ys = []
    for m_ in range(S_WIDTH // MXU_TILE):
        cols = slice(MXU_TILE * m_, MXU_TILE * (m_ + 1))
        re = slice(half * m_, half * (m_ + 1))
        im = slice(N_STATE + half * m_, N_STATE + half * (m_ + 1))
        ys.append(_dot(hist[:, re].astype(BF16), cmat_ref[re, cols])
                  + _dot(hist[:, im].astype(BF16), cmat_ref[im, cols]))
    y = jnp.concatenate(ys, axis=1) + d_ref[...] * u
    y_ref[...] = _gelu_tanh(y).astype(BF16).reshape(y_ref.shape)
    ht_ref[...] = state[...]


def _s5(u_tm, h0, a_row, bmat, cmat, d_row, *, tc):
    nt, bs, _ = u_tm.shape
    rb = 16 if bs % 16 == 0 else 8
    assert bs % rb == 0 and nt % tc == 0
    return pl.pallas_call(
        functools.partial(_s5_kernel, tc=tc, bs=bs, rb=rb, lc=256),
        grid=(nt // tc,),
        in_specs=[pl.BlockSpec((tc, bs, S_WIDTH), lambda t: (t, 0, 0)),
                  _const_spec((bs, 2 * N_STATE)), _const_spec((1, 2 * N_STATE)),
                  _const_spec((S_WIDTH, 2 * N_STATE)), _const_spec((2 * N_STATE, S_WIDTH)),
                  _const_spec((1, S_WIDTH))],
        out_specs=[pl.BlockSpec((tc, bs, S_WIDTH), lambda t: (t, 0, 0)),
                   pl.BlockSpec((bs, 2 * N_STATE), lambda t: (0, 0))],
        out_shape=[jax.ShapeDtypeStruct((nt, bs, S_WIDTH), BF16),
                   jax.ShapeDtypeStruct((bs, 2 * N_STATE), F32)],
        scratch_shapes=[pltpu.VMEM((tc * bs, 2 * N_STATE), F32), pltpu.VMEM((bs, 2 * N_STATE), F32)],
        compiler_params=_cparams("arbitrary"),
        name="s5",
    )(u_tm, h0, a_row, bmat, cmat, d_row)


def _post_kernel(x_ref, a_ref, ys_ref, sga_ref, sgs_ref, g1_ref, sh2_ref, sc2_ref, g2_ref, gffn_ref,
                 wglu_ref, bglu_ref, wupa_ref, wups_ref, wout_ref, wffu_ref, wffd_ref, y_ref,
                 *, bb, tt, ffc):
    rows = bb * tt
    ys = ys_ref[...].reshape(rows, S_WIDTH)
    glu = _dot(ys, wglu_ref[...]) + bglu_ref[...]
    s = (ys.astype(F32) * jax.nn.sigmoid(glu)).astype(BF16)
    a = a_ref[...].reshape(rows, A_WIDTH)
    merged = (sga_ref[...].reshape(rows, D_MODEL).astype(F32) * _dot(a, wupa_ref[...])
              + sgs_ref[...].reshape(rows, D_MODEL).astype(F32) * _dot(s, wups_ref[...]))
    mix = _dot(merged.astype(BF16), wout_ref[...]).reshape(bb, tt, D_MODEL)
    x1 = x_ref[...] + g1_ref[...] * mix

    ms = jnp.mean(x1 * x1, axis=-1, keepdims=True)
    h2 = x1 * lax.rsqrt(ms + NORM_EPS) * gffn_ref[...]
    h2 = (h2 * (1.0 + sc2_ref[...]) + sh2_ref[...]).reshape(rows, D_MODEL).astype(BF16)
    down = jnp.zeros((rows, D_MODEL), F32)
    for c in range(D_FF // ffc):
        up = jnp.maximum(_dot(h2, wffu_ref[:, ffc * c:ffc * (c + 1)]), 0.0)
        down = down + _dot((up * up).astype(BF16), wffd_ref[ffc * c:ffc * (c + 1), :])
    y_ref[...] = x1 + g2_ref[...] * down.reshape(bb, tt, D_MODEL)


def _post(x, a, ys, sga, sgs, g1, sh2, sc2, g2, gffn, wglu, bglu, wupa, wups, wout, wffu, wffd,
          *, bb, tt, ys_time_major):
    nb, nt_total, _ = x.shape

    def tok(width):
        return pl.BlockSpec((bb, tt, width), lambda b, t: (b, t, 0))

    def mod():
        return pl.BlockSpec((bb, 1, D_MODEL), lambda b, t: (b, 0, 0))

    if ys_time_major:
        assert bb == 1
        ys_spec = pl.BlockSpec((tt, S_WIDTH), lambda b, t: (t, b))
    else:
        ys_spec = tok(S_WIDTH)

    return pl.pallas_call(
        functools.partial(_post_kernel, bb=bb, tt=tt, ffc=1024),
        grid=(nb // bb, nt_total // tt),
        in_specs=[tok(D_MODEL), tok(A_WIDTH), ys_spec, tok(D_MODEL), tok(D_MODEL),
                  mod(), mod(), mod(), mod(), _const_spec((1, D_MODEL)),
                  _const_spec((S_WIDTH, S_WIDTH)), _const_spec((1, S_WIDTH)),
                  _const_spec((A_WIDTH, D_MODEL)), _const_spec((S_WIDTH, D_MODEL)),
                  _const_spec((D_MODEL, D_MODEL)), _const_spec((D_MODEL, D_FF)), _const_spec((D_FF, D_MODEL))],
        out_specs=tok(D_MODEL),
        out_shape=jax.ShapeDtypeStruct((nb, nt_total, D_MODEL), F32),
        compiler_params=_cparams("parallel", "parallel"),
        name="post",
    )(x, a, ys, sga, sgs, g1, sh2, sc2, g2, gffn, wglu, bglu, wupa, wups, wout, wffu, wffd)


def _block_diag_heads(q, width):
    nb, ts, _ = q.shape
    q5 = q.reshape(nb, ts, 1, N_HEADS, width)
    keep = jnp.eye(N_HEADS, dtype=bool).reshape(1, 1, N_HEADS, N_HEADS, 1)
    return jnp.where(keep, q5, jnp.zeros((), q.dtype)).reshape(nb, ts * N_HEADS, N_HEADS * width)


def kernel(x_prompt, x_sample, cache_k, cache_v, cache_logf, state_ssm_re, state_ssm_im, page_table,
           c_prompt, c_sample, w_ada, b_ada, norm_mix_g, norm_ffn_g, w_in, b_fgate, q_norm_g, k_norm_g,
           ssm_lambda_re, ssm_lambda_im, ssm_log_dt, ssm_b_re, ssm_b_im, ssm_c_re, ssm_c_im, ssm_d,
           w_glu, b_glu, w_up_a, w_up_s, w_out, w_ffn_up, w_ffn_down):
    nbp, ntp, _ = x_prompt.shape
    nbs, nts, _ = x_sample.shape
    n_pool = cache_k.shape[0]

    a3 = 3 * A_WIDTH
    wf_pad = jnp.pad(w_in[:, a3:a3 + N_HEADS], ((0, 0), (0, LANES - N_HEADS)))
    w2 = jnp.concatenate([w_in[:, :a3], w_in[:, a3 + N_HEADS:], wf_pad], axis=1).astype(BF16)
    bfp = jnp.pad(b_fgate, (0, LANES - N_HEADS)).reshape(1, LANES)
    wt = jnp.concatenate([w_in[:, :A_WIDTH].T, w_in[:, 2 * A_WIDTH:a3].T], axis=0).astype(BF16)
    gqp = jnp.pad(q_norm_g, (0, LANES - D_HEAD)).reshape(1, LANES)
    gq_col = q_norm_g.reshape(D_HEAD, 1)
    gkp = jnp.pad(k_norm_g, (0, LANES - D_HEAD)).reshape(1, LANES)
    heads = jnp.arange(N_HEADS)
    e2k = jnp.zeros((LANES, QA_WIDTH), F32)
    cq = jnp.zeros((1, QA_WIDTH), F32)
    for piece in range(N_SPLIT):
        e2k = e2k.at[piece * N_HEADS + heads, HEAD_PAD * heads + D_HEAD + piece].set(-1.0)
        cq = cq.at[0, HEAD_PAD * heads + D_HEAD + piece].set(1.0)
    e2k = e2k.astype(BF16)
    gmix = norm_mix_g.reshape(1, D_MODEL)
    gffn = norm_ffn_g.reshape(1, D_MODEL)
    post_w = (w_glu.astype(BF16), b_glu.reshape(1, S_WIDTH), w_up_a.astype(BF16), w_up_s.astype(BF16),
              w_out.astype(BF16), w_ffn_up.astype(BF16), w_ffn_down.astype(BF16))

    mod = _adaln(jnp.concatenate([c_prompt, c_sample], axis=0), w_ada.astype(BF16), b_ada)
    mod = mod.reshape(nbp + nbs, 1, 6, D_MODEL)
    mods_p = [mod[:nbp, :, i, :] for i in range(6)]
    mods_s = [mod[nbp:, :, i, :] for i in range(6)]

    ab_re, ab_im, bb_re, bb_im = _s5_disc(ssm_lambda_re, ssm_lambda_im, ssm_log_dt, ssm_b_re, ssm_b_im)
    a_row = jnp.concatenate([ab_re.reshape(1, N_STATE), ab_im.reshape(1, N_STATE)], axis=1)
    eye_g = jnp.eye(N_GROUPS, dtype=F32)

    def b_blockdiag(bb):
        return jnp.einsum('gcp,gh->gchp', bb, eye_g).reshape(S_WIDTH, N_STATE)

    def c_blockdiag(cc):
        return jnp.einsum('gcp,gh->gphc', cc, eye_g).reshape(N_STATE, S_WIDTH)

    bmat = jnp.concatenate([b_blockdiag(bb_re), b_blockdiag(bb_im)], axis=1).astype(BF16)
    cmat = jnp.concatenate([c_blockdiag(ssm_c_re), -c_blockdiag(ssm_c_im)], axis=0).astype(BF16)
    d_row = ssm_d.reshape(1, S_WIDTH)

    tt = min(512, ntp)
    ablk = min(512, ntp)
    qat, ka, k_p, vt_p, vtb, lf_p, u_tm, sga, sgs = _inproj(
        x_prompt, mods_p[0], mods_p[1], gmix, w2, wt, bfp, gqp, gq_col, gkp, e2k, cq,
        bb=1, tt=tt, prompt=True, kblk=ablk)
    attn_p = _attn(qat, ka, vtb, qblk=min(512, ntp), kblk=ablk)
    v_p = jnp.transpose(vt_p.reshape(nbp, N_HEADS, D_HEAD, ntp), (0, 3, 1, 2))
    h0 = jnp.zeros((nbp, 2 * N_STATE), F32)
    ys_tm, ht_p = _s5(u_tm.reshape(ntp, nbp, S_WIDTH), h0, a_row, bmat, cmat, d_row, tc=min(32, ntp))
    y_prompt = _post(x_prompt, attn_p, ys_tm.reshape(ntp, nbp * S_WIDTH), sga, sgs,
                     mods_p[2], mods_p[3], mods_p[4], mods_p[5], gffn, *post_w,
                     bb=1, tt=tt, ys_time_major=True)

    bbs = min(64, nbs)
    qa_s, ka_s, k_s, v_s, vb_s, lf_s, u_s, sga_s, sgs_s, qn_s = _inproj(
        x_sample, mods_s[0], mods_s[1], gmix, w2, wt, bfp, gqp, gq_col, gkp, e2k, cq,
        bb=bbs, tt=nts, prompt=False, kblk=ablk)
    lf_rows = jnp.swapaxes(cache_logf, 1, 2).reshape(n_pool * N_HEADS, PAGE)
    rsuf = _pool_logf(lf_rows).reshape(n_pool, N_HEADS, 2 * PAGE)
    cache_kt = jnp.transpose(cache_k, (0, 2, 3, 1)).reshape(n_pool, A_WIDTH, PAGE)
    cache_vt = jnp.transpose(cache_v, (0, 2, 3, 1)).reshape(n_pool, A_WIDTH, PAGE)
    attn_s = _sattn(page_table, _block_diag_heads(qn_s, D_HEAD), _block_diag_heads(qa_s, HEAD_PAD),
                    ka_s, vb_s, cache_kt, cache_vt, rsuf, pp=min(16, page_table.shape[1]))
    h0_s = jnp.concatenate([state_ssm_re.reshape(nbs, N_STATE), state_ssm_im.reshape(nbs, N_STATE)], axis=1)
    ys_s_tm, ht_s = _s5(jnp.swapaxes(u_s, 0, 1), h0_s, a_row, bmat, cmat, d_row, tc=nts)
    y_sample = _post(x_sample, attn_s, jnp.swapaxes(ys_s_tm, 0, 1), sga_s, sgs_s,
                     mods_s[2], mods_s[3], mods_s[4], mods_s[5], gffn, *post_w,
                     bb=bbs, tt=nts, ys_time_major=False)

    def heads4(z):
        return z.reshape(z.shape[0], z.shape[1], N_HEADS, D_HEAD)

    def state3(z):
        return z.reshape(z.shape[0], N_GROUPS, STATE_P)

    return (y_prompt, y_sample, heads4(k_p), heads4(v_p), lf_p,
            state3(ht_p[:, :N_STATE]), state3(ht_p[:, N_STATE:]),
            heads4(k_s), heads4(v_s), lf_s,
            state3(ht_s[:, :N_STATE]), state3(ht_s[:, N_STATE:]))
```

```python
import functools
import math

import jax
import jax.numpy as jnp
from jax import lax
from jax.experimental import pallas as pl
from jax.experimental.pallas import tpu as pltpu

F32 = jnp.float32
BF16 = jnp.bfloat16

D_MODEL = 1024
N_HEADS = 8
D_HEAD = 64
A_WIDTH = N_HEADS * D_HEAD
S_WIDTH = D_MODEL // 2
GROUP_CH = 16
N_GROUPS = S_WIDTH // GROUP_CH
STATE_P = 64
N_STATE = N_GROUPS * STATE_P
D_FF = 4 * D_MODEL
PAGE = 128
ATTN_SCALE = 1.0 / math.sqrt(D_HEAD)
LOG2_E = math.log2(math.e)
NORM_EPS = 1e-6
NEG_BIG = -1e30

LANES = 128
MXU_TILE = 256
HEAD_PAD = 128
QA_WIDTH = N_HEADS * HEAD_PAD
N_SPLIT = 3
VMEM_LIMIT = 56 * 1024 * 1024
POST_VMEM_LIMIT = 60 * 1024 * 1024

_OQ, _OK, _OV, _OU, _OGA, _OGS, _OF, _W2 = 0, 512, 1024, 1536, 2048, 3072, 4096, 4224


def _cparams(*sem):
    return pltpu.CompilerParams(dimension_semantics=sem, vmem_limit_bytes=VMEM_LIMIT)


def _const_spec(shape):
    nd = len(shape)
    return pl.BlockSpec(shape, lambda *_: (0,) * nd, pipeline_mode=pl.Buffered(1))


def _split3(x):
    hi = x.astype(BF16)
    r1 = x - hi.astype(F32)
    mid = r1.astype(BF16)
    lo = (r1 - mid.astype(F32)).astype(BF16)
    return hi, mid, lo


def _dot(a, b):
    return jnp.dot(a, b, preferred_element_type=F32)


def _dot_nt(a, b):
    return lax.dot_general(a, b, (((1,), (1,)), ((), ())), preferred_element_type=F32)


def _adaln_kernel(c_ref, w_ref, b_ref, o_ref):
    c = c_ref[...]
    s = (c * jax.nn.sigmoid(c)).astype(BF16)
    o_ref[...] = _dot(s, w_ref[...]) + b_ref[...]


def _adaln(c_all, w_ada_bf, b_ada):
    n = c_all.shape[0]
    tn = 1024
    return pl.pallas_call(
        _adaln_kernel,
        grid=(6 * D_MODEL // tn,),
        in_specs=[pl.BlockSpec((n, D_MODEL), lambda j: (0, 0)),
                  pl.BlockSpec((D_MODEL, tn), lambda j: (0, j)),
                  pl.BlockSpec((1, tn), lambda j: (0, j))],
        out_specs=pl.BlockSpec((n, tn), lambda j: (0, j)),
        out_shape=jax.ShapeDtypeStruct((n, 6 * D_MODEL), F32),
        compiler_params=_cparams("parallel"),
        name="adaln",
    )(c_all, w_ada_bf, b_ada.reshape(1, -1))


def _s5_disc_kernel(lr_ref, li_ref, ldt_ref, br_ref, bi_ref, abr_ref, abi_ref, bbr_ref, bbi_ref):
    lr = lr_ref[...]
    li = li_ref[...]
    dt = jnp.exp(ldt_ref[...])
    mag = jnp.exp(lr * dt)
    ab_re = mag * jnp.cos(li * dt)
    ab_im = mag * jnp.sin(li * dt)
    nr, ni = ab_re - 1.0, ab_im
    den = lr * lr + li * li
    f_re = (nr * lr + ni * li) / den
    f_im = (ni * lr - nr * li) / den
    br = br_ref[...]
    bi = bi_ref[...]
    abr_ref[...] = ab_re
    abi_ref[...] = ab_im
    bbr_ref[...] = f_re * br - f_im * bi
    bbi_ref[...] = f_re * bi + f_im * br


def _s5_disc(lam_re, lam_im, log_dt, b_re, b_im):
    g3 = jax.ShapeDtypeStruct((N_GROUPS, 1, STATE_P), F32)
    b3 = jax.ShapeDtypeStruct((N_GROUPS, GROUP_CH, STATE_P), F32)
    return pl.pallas_call(
        _s5_disc_kernel,
        out_shape=(g3, g3, b3, b3),
        name="s5_disc",
    )(lam_re.reshape(N_GROUPS, 1, STATE_P), lam_im.reshape(N_GROUPS, 1, STATE_P),
      log_dt.reshape(N_GROUPS, 1, 1),
      jnp.swapaxes(b_re, 1, 2), jnp.swapaxes(b_im, 1, 2))


def _inproj_kernel(x_ref, sh_ref, sc_ref, gmix_ref, w_ref, wt_ref, bf_ref, gq_ref, gqcol_ref, gk_ref,
                   e2k_ref, cq_ref, tri_ref, *rest, bb, tt, prompt, kblk):
    if prompt:
        qat_ref, ka_ref, k_ref, vt_ref, vtb_ref, lf_ref, u_ref, sga_ref, sgs_ref, carry_ref = rest
    else:
        qa_ref, ka_ref, k_ref, v_ref, vb_ref, lf_ref, u_ref, sga_ref, sgs_ref, qn_ref, carry_ref = rest
    rows = bb * tt
    t = pl.program_id(1)

    x = x_ref[...]
    ms = jnp.mean(x * x, axis=-1, keepdims=True)
    h = x * lax.rsqrt(ms + NORM_EPS) * gmix_ref[...]
    h = h * (1.0 + sc_ref[...]) + sh_ref[...]
    hb = h.reshape(rows, D_MODEL).astype(BF16)

    lane = lax.broadcasted_iota(jnp.int32, (1, LANES), 1)
    low_half = lane < D_HEAD

    zf = _dot(hb, w_ref[:, _OF:_W2]) + bf_ref[...]
    lf = jnp.minimum(zf, 0.0) - jnp.log1p(jnp.exp(-jnp.abs(zf)))
    lf = jnp.where(lane < N_HEADS, lf, 0.0)
    lf_ref[...] = lf[:, :N_HEADS].reshape(lf_ref.shape)
    hi, mid, lo = _split3(lf)
    packed = (hi.astype(F32) + pltpu.roll(mid.astype(F32), N_HEADS, 1)
              + pltpu.roll(lo.astype(F32), 2 * N_HEADS, 1)).astype(BF16)
    gc = _dot(tri_ref[...], packed)
    g = gc + pltpu.roll(gc, LANES - N_HEADS, 1) + pltpu.roll(gc, LANES - 2 * N_HEADS, 1)
    g = jnp.where(lane < N_HEADS, g, 0.0)
    if bb == 1:
        @pl.when(t == 0)
        def _():
            carry_ref[...] = jnp.zeros_like(carry_ref)
        g = g + carry_ref[...]
        carry_ref[...] = g[rows - 1:rows, :]
    logit_scale = LOG2_E if prompt else 1.0
    ghi, gmid, glo = _split3(g * logit_scale)
    gpacked = (ghi.astype(F32) + pltpu.roll(gmid.astype(F32), N_HEADS, 1)
               + pltpu.roll(glo.astype(F32), 2 * N_HEADS, 1)).astype(BF16)
    kaug = _dot(gpacked, e2k_ref[...])

    def widen_norm(z, gain):
        out = []
        for part in (z, pltpu.roll(z, D_HEAD, 1)):
            e = jnp.where(low_half, part, 0.0)
            ss = jnp.sum(e * e, axis=-1, keepdims=True)
            out.append(e * lax.rsqrt(ss * (1.0 / D_HEAD) + NORM_EPS) * gain)
        return out

    gq = gq_ref[...] * ATTN_SCALE
    gk = gk_ref[...]
    for j in range(N_HEADS // 2):
        zk = _dot(hb, w_ref[:, _OK + LANES * j:_OK + LANES * (j + 1)])
        k_even, k_odd = widen_norm(zk, gk)
        for i, kh in enumerate((k_even, k_odd)):
            c0 = HEAD_PAD * (2 * j + i)
            ka_ref[..., c0:c0 + HEAD_PAD] = (kh + kaug[:, c0:c0 + HEAD_PAD]).astype(BF16).reshape(bb, tt, HEAD_PAD)
        k_ref[..., LANES * j:LANES * (j + 1)] = (k_even + pltpu.roll(k_odd, D_HEAD, 1)).reshape(bb, tt, LANES)
        if not prompt:
            zq = _dot(hb, w_ref[:, _OQ + LANES * j:_OQ + LANES * (j + 1)])
            q_even, q_odd = widen_norm(zq, gq)
            for i, qh in enumerate((q_even, q_odd)):
                c0 = HEAD_PAD * (2 * j + i)
                qa_ref[..., c0:c0 + HEAD_PAD] = (
                    (qh + cq_ref[:, c0:c0 + HEAD_PAD]).astype(BF16).reshape(bb, tt, HEAD_PAD))
            qn_ref[..., LANES * j:LANES * (j + 1)] = (
                (q_even + pltpu.roll(q_odd, D_HEAD, 1)).astype(BF16).reshape(bb, tt, LANES))

    if prompt:
        zqt = _dot_nt(wt_ref[:A_WIDTH, :], hb)
        gq_col = gqcol_ref[...] * (ATTN_SCALE * logit_scale)
        sub = lax.broadcasted_iota(jnp.int32, (D_HEAD, tt), 0)
        ones_rows = jnp.where(sub < N_SPLIT, 1.0, 0.0).astype(BF16)
        for hd in range(N_HEADS):
            z = zqt[D_HEAD * hd:D_HEAD * (hd + 1), :]
            ss = jnp.sum(z * z, axis=0, keepdims=True)
            qn_t = z * lax.rsqrt(ss * (1.0 / D_HEAD) + NORM_EPS) * gq_col
            qat_ref[0, HEAD_PAD * hd:HEAD_PAD * hd + D_HEAD, :] = qn_t.astype(BF16)
            qat_ref[0, HEAD_PAD * hd + D_HEAD:HEAD_PAD * (hd + 1), :] = ones_rows
        zvt = _dot_nt(wt_ref[A_WIDTH:, :], hb)
        vt_ref[0] = zvt
        for c in range(tt // kblk):
            vtb_ref[0, c] = zvt[:, kblk * c:kblk * (c + 1)].astype(BF16)
    else:
        zv = _dot(hb, w_ref[:, _OV:_OU])
        v_ref[...] = zv.reshape(v_ref.shape)
        vb_ref[...] = zv.astype(BF16).reshape(vb_ref.shape)
    u_ref[...] = _dot(hb, w_ref[:, _OU:_OGA]).reshape(u_ref.shape)
    sga_ref[...] = jax.nn.sigmoid(_dot(hb, w_ref[:, _OGA:_OGS])).astype(BF16).reshape(sga_ref.shape)
    sgs_ref[...] = jax.nn.sigmoid(_dot(hb, w_ref[:, _OGS:_OF])).astype(BF16).reshape(sgs_ref.shape)


def _inproj(x, sh1, sc1, gmix, w2, wt, bfp, gqp, gq_col, gkp, e2k, cq, *, bb, tt, prompt, kblk):
    nb, nt_total, _ = x.shape
    u_time_major = prompt
    rows = bb * tt
    idx = jnp.arange(rows)
    tri = ((idx[:, None] >= idx[None, :]) & (idx[:, None] // tt == idx[None, :] // tt)).astype(BF16)
    grid = (nb // bb, nt_total // tt)
    assert bb == 1 or grid[1] == 1

    def tok(width):
        return pl.BlockSpec((bb, tt, width), lambda b, t: (b, t, 0))

    def mod():
        return pl.BlockSpec((bb, 1, D_MODEL), lambda b, t: (b, 0, 0))

    if u_time_major:
        assert bb == 1
        u_shape = jax.ShapeDtypeStruct((nt_total, nb * S_WIDTH), F32)
        u_spec = pl.BlockSpec((tt, S_WIDTH), lambda b, t: (t, b))
    else:
        u_shape = jax.ShapeDtypeStruct((nb, nt_total, S_WIDTH), F32)
        u_spec = tok(S_WIDTH)

    def sds(width, dt):
        return jax.ShapeDtypeStruct((nb, nt_total, width), dt)

    tail_shape = [sds(N_HEADS, F32), u_shape, sds(D_MODEL, BF16), sds(D_MODEL, BF16)]
    tail_specs = [tok(N_HEADS), u_spec, tok(D_MODEL), tok(D_MODEL)]
    if prompt:
        assert tt % kblk == 0
        out_shape = [jax.ShapeDtypeStruct((nb, QA_WIDTH, nt_total), BF16), sds(QA_WIDTH, BF16), sds(A_WIDTH, F32),
                     jax.ShapeDtypeStruct((nb, A_WIDTH, nt_total), F32),
                     jax.ShapeDtypeStruct((nb, nt_total // kblk, A_WIDTH, kblk), BF16)] + tail_shape
        out_specs = [pl.BlockSpec((1, QA_WIDTH, tt), lambda b, t: (b, 0, t)), tok(QA_WIDTH), tok(A_WIDTH),
                     pl.BlockSpec((1, A_WIDTH, tt), lambda b, t: (b, 0, t)),
                     pl.BlockSpec((1, tt // kblk, A_WIDTH, kblk), lambda b, t: (b, t, 0, 0))] + tail_specs
    else:
        out_shape = ([sds(QA_WIDTH, BF16), sds(QA_WIDTH, BF16), sds(A_WIDTH, F32), sds(A_WIDTH, F32),
                      sds(A_WIDTH, BF16)] + tail_shape + [sds(A_WIDTH, BF16)])
        out_specs = ([tok(QA_WIDTH), tok(QA_WIDTH), tok(A_WIDTH), tok(A_WIDTH), tok(A_WIDTH)]
                     + tail_specs + [tok(A_WIDTH)])

    return pl.pallas_call(
        functools.partial(_inproj_kernel, bb=bb, tt=tt, prompt=prompt, kblk=kblk),
        grid=grid,
        in_specs=[tok(D_MODEL), mod(), mod(), _const_spec((1, D_MODEL)), _const_spec((D_MODEL, _W2)),
                  _const_spec((2 * A_WIDTH, D_MODEL)),
                  _const_spec((1, LANES)), _const_spec((1, LANES)), _const_spec((D_HEAD, 1)), _const_spec((1, LANES)),
                  _const_spec((LANES, QA_WIDTH)), _const_spec((1, QA_WIDTH)), _const_spec((rows, rows))],
        out_specs=out_specs,
        out_shape=out_shape,
        scratch_shapes=[pltpu.VMEM((1, LANES), F32)],
        compiler_params=_cparams("parallel", "arbitrary"),
        name="inproj",
    )(x, sh1, sc1, gmix, w2, wt, bfp, gqp, gq_col, gkp, e2k, cq, tri)


def _attn_kernel(qat_ref, ka_ref, vtb_ref, o_ref, m_sc, l_sc, acc_sc, *, qblk, kblk):
    qi = pl.program_id(1)
    ratio = qblk // kblk
    m_sc[...] = jnp.full_like(m_sc, NEG_BIG)
    l_sc[...] = jnp.zeros_like(l_sc)
    acc_sc[...] = jnp.zeros_like(acc_sc)
    key = lax.broadcasted_iota(jnp.int32, (kblk, qblk), 0)
    qry = lax.broadcasted_iota(jnp.int32, (kblk, qblk), 1)

    def block(kb, diag):
        ks = pl.multiple_of(kb * kblk, kblk)
        for h in range(N_HEADS):
            k = ka_ref[0, pl.ds(ks, kblk), HEAD_PAD * h:HEAD_PAD * (h + 1)]
            st = _dot(k, qat_ref[0, HEAD_PAD * h:HEAD_PAD * (h + 1), :])
            if diag is not None:
                st = jnp.where(key + diag * kblk <= qry, st, NEG_BIG)
            m_old = m_sc[h]
            m_new = jnp.maximum(m_old, jnp.max(st, axis=0, keepdims=True))
            alpha = jnp.exp2(m_old - m_new)
            p = jnp.exp2(st - m_new)
            l_sc[h] = alpha * l_sc[h] + jnp.sum(p, axis=0, keepdims=True)
            vt = vtb_ref[0, kb, D_HEAD * h:D_HEAD * (h + 1), :]
            rows = slice(D_HEAD * h, D_HEAD * (h + 1))
            acc_sc[rows, :] = alpha * acc_sc[rows, :] + _dot(vt, p.astype(BF16))
            m_sc[h] = m_new

    def body(kb, carry):
        block(kb, None)
        return carry

    lax.fori_loop(0, qi * ratio, body, 0)
    for d in range(ratio):
        block(qi * ratio + d, d)

    for hp in range(N_HEADS // 2):
        inv = jnp.concatenate([jnp.broadcast_to(1.0 / l_sc[2 * hp + i], (D_HEAD, qblk)) for i in range(2)], axis=0)
        pair = acc_sc[LANES * hp:LANES * (hp + 1), :] * inv
        o_ref[0, :, LANES * hp:LANES * (hp + 1)] = pair.T.astype(BF16)


def _attn(qat, ka, vtb, *, qblk, kblk):
    nb, nt, _ = ka.shape
    assert vtb.shape == (nb, nt // kblk, A_WIDTH, kblk) and qblk % kblk == 0
    return pl.pallas_call(
        functools.partial(_attn_kernel, qblk=qblk, kblk=kblk),
        grid=(nb, nt // qblk),
        in_specs=[pl.BlockSpec((1, QA_WIDTH, qblk), lambda b, i: (b, 0, i)),
                  pl.BlockSpec((1, nt, QA_WIDTH), lambda b, i: (b, 0, 0)),
                  pl.BlockSpec((1, nt // kblk, A_WIDTH, kblk), lambda b, i: (b, 0, 0, 0))],
        out_specs=pl.BlockSpec((1, qblk, A_WIDTH), lambda b, i: (b, i, 0)),
        out_shape=jax.ShapeDtypeStruct((nb, nt, A_WIDTH), BF16),
        scratch_shapes=[pltpu.VMEM((N_HEADS, 1, qblk), F32), pltpu.VMEM((N_HEADS, 1, qblk), F32),
                        pltpu.VMEM((A_WIDTH, qblk), F32)],
        compiler_params=_cparams("parallel", "arbitrary"),
        name="attn",
    )(qat, ka, vtb)


def _pool_logf_kernel(lf_ref, m_ref, o_ref):
    hi, mid, lo = _split3(lf_ref[...])
    m = m_ref[...]
    o_ref[...] = _dot(hi, m) + _dot(mid, m) + _dot(lo, m)


def _pool_logf(lf_rows):
    n_rows = lf_rows.shape[0]
    tile = next(c for c in (2048, 1024, 512, 256, 128, 64, 32, 16, 8) if n_rows % c == 0)
    j_src = jnp.arange(PAGE)
    suffix = j_src[:, None] > j_src[None, :]
    m = jnp.concatenate([suffix, jnp.ones((PAGE, PAGE), bool)], axis=1).astype(BF16)
    return pl.pallas_call(
        _pool_logf_kernel,
        grid=(n_rows // tile,),
        in_specs=[pl.BlockSpec((tile, PAGE), lambda i: (i, 0)), _const_spec((PAGE, 2 * PAGE))],
        out_specs=pl.BlockSpec((tile, 2 * PAGE), lambda i: (i, 0)),
        out_shape=jax.ShapeDtypeStruct((n_rows, 2 * PAGE), F32),
        compiler_params=_cparams("parallel"),
        name="pool_logf",
    )(lf_rows, m)


def _sattn_chunk(g, idx, refs, *, pp, ts, n_pages, cps):
    (pt_ref, qbd_ref, qabd_ref, kan_ref, vn_ref, k_hbm, v_hbm, r_hbm, o_ref,
     kpage, vpage, rpage, sem, kbuf, vbuf, m_sc, l_sc, acc_sc, tot_sc) = refs
    cpq = n_pages // pp
    spp = cps // cpq
    n_steps = pl.num_programs(0) * pl.num_programs(1)
    nrow = N_HEADS * ts

    def page_copies(g_, idx_):
        seq, chunk, slot_ = g_ * spp + idx_ // cpq, idx_ % cpq, idx_ % 2
        out = []
        for i in range(pp):
            pid = pt_ref[seq * n_pages + (n_pages - 1 - (chunk * pp + i))]
            out.append(pltpu.make_async_copy(k_hbm.at[pid], kpage.at[slot_, i], sem.at[0, slot_]))
            out.append(pltpu.make_async_copy(v_hbm.at[pid], vpage.at[slot_, i], sem.at[1, slot_]))
            out.append(pltpu.make_async_copy(r_hbm.at[pid], rpage.at[slot_, i], sem.at[2, slot_]))
        return out

    if idx == 0:
        @pl.when(g == 0)
        def _():
            for cp in page_copies(g, 0):
                cp.start()
    if idx + 1 < cps:
        for cp in page_copies(g, idx + 1):
            cp.start()
    else:
        @pl.when(g + 1 < n_steps)
        def _():
            for cp in page_copies(g + 1, 0):
                cp.start()
    for cp in page_copies(g, idx):
        cp.wait()

    slot, seq_l, chunk = idx % 2, idx // cpq, idx % cpq
    if chunk == 0:
        m_sc[...] = jnp.full_like(m_sc, NEG_BIG)
        l_sc[...] = jnp.zeros_like(l_sc)
        acc_sc[...] = jnp.zeros_like(acc_sc)
        tot_sc[...] = jnp.zeros_like(tot_sc)

    for i in range(pp):
        kbuf[:, PAGE * i:PAGE * (i + 1)] = kpage[slot, i].astype(BF16)
        vbuf[:, PAGE * i:PAGE * (i + 1)] = vpage[slot, i].astype(BF16)

    def update(s, pv):
        m_old = m_sc[...]
        m_new = jnp.maximum(m_old, jnp.max(s, axis=-1, keepdims=True))
        alpha = jnp.exp(m_old - m_new)
        p = jnp.exp(s - m_new)
        l_sc[...] = alpha * l_sc[...] + jnp.sum(p, axis=-1, keepdims=True)
        acc_sc[...] = alpha * acc_sc[...] + pv(p.astype(BF16))
        m_sc[...] = m_new

    later = tot_sc[...]
    biases = []
    for i in range(pp):
        r = rpage[slot, i]
        biases.append(jnp.concatenate([r[:, :PAGE] + later] * ts, axis=0))
        later = later + r[:, PAGE:]
    tot_sc[...] = later
    s = _dot(qbd_ref[seq_l], kbuf[...]) + jnp.concatenate(biases, axis=1)
    update(s, lambda p: _dot_nt(p, vbuf[...]))

    if chunk == cpq - 1:
        s_new = _dot_nt(qabd_ref[seq_l], kan_ref[seq_l])
        r_i = lax.broadcasted_iota(jnp.int32, (nrow, ts), 0)
        c_i = lax.broadcasted_iota(jnp.int32, (nrow, ts), 1)
        update(jnp.where((r_i // N_HEADS) >= c_i, s_new, NEG_BIG), lambda p: _dot(p, vn_ref[seq_l]))
        o = acc_sc[...] / l_sc[...]
        lane_head = lax.broadcasted_iota(jnp.int32, (N_HEADS, A_WIDTH), 1) // D_HEAD
        own = lane_head == lax.broadcasted_iota(jnp.int32, (N_HEADS, A_WIDTH), 0)
        rows = [jnp.sum(jnp.where(own, o[N_HEADS * t:N_HEADS * (t + 1), :], 0.0), axis=0, keepdims=True)
                for t in range(ts)]
        o_ref[seq_l] = jnp.concatenate(rows, axis=0).astype(BF16)


def _gelu_tanh(x):
    return 0.5 * x * (1.0 + jnp.tanh(math.sqrt(2.0 / math.pi) * (x + 0.044715 * (x * x * x))))


def _s5_kernel(u_ref, h0_ref, a_ref, bmat_ref, cmat_ref, d_ref, y_ref, ht_ref, hist, state,
               *, tc, bs, rb, lc):
    t = pl.program_id(0)

    @pl.when(t == 0)
    def _():
        state[...] = h0_ref[...]

    u = u_ref[...].reshape(tc * bs, S_WIDTH)
    ub = u.astype(BF16)
    for n in range(2 * N_STATE // MXU_TILE):
        slab = LANES * ((n % (N_STATE // MXU_TILE)) // 2)
        cols = slice(MXU_TILE * n, MXU_TILE * (n + 1))
        hist[:, cols] = _dot(ub[:, slab:slab + LANES], bmat_ref[slab:slab + LANES, cols])

    for c in range(N_STATE // lc):
        re = slice(lc * c, lc * (c + 1))
        im = slice(N_STATE + lc * c, N_STATE + lc * (c + 1))
        a_re = jnp.broadcast_to(a_ref[:, re], (rb, lc))
        a_im = jnp.broadcast_to(a_ref[:, im], (rb, lc))

        def row_block(r, _, re=re, im=im, a_re=a_re, a_im=a_im):
            r0 = pl.multiple_of(r * rb, rb)

            def step(j, carry):
                h_re, h_im = carry
                row = pl.multiple_of(j * bs + r0, rb)
                n_re = a_re * h_re - a_im * h_im + hist[pl.ds(row, rb), re]
                n_im = a_re * h_im + a_im * h_re + hist[pl.ds(row, rb), im]
                hist[pl.ds(row, rb), re] = n_re
                hist[pl.ds(row, rb), im] = n_im
                return n_re, n_im

            h_re, h_im = lax.fori_loop(0, tc, step, (state[pl.ds(r0, rb), re], state[pl.ds(r0, rb), im]))
            state[pl.ds(r0, rb), re] = h_re
            state[pl.ds(r0, rb), im] = h_im
            return 0

        lax.fori_loop(0, bs // rb, row_block, 0)

    half = N_STATE // (S_WIDTH // MXU_TILE)
    ys = []
    for m_ in range(S_WIDTH // MXU_TILE):
        cols = slice(MXU_TILE * m_, MXU_TILE * (m_ + 1))
        re = slice(half * m_, half * (m_ + 1))
        im = slice(N_STATE + half * m_, N_STATE + half * (m_ + 1))
        ys.append(_dot(hist[:, re].astype(BF16), cmat_ref[re, cols])
                  + _dot(hist[:, im].astype(BF16), cmat_ref[im, cols]))
    y = jnp.concatenate(ys, axis=1) + d_ref[...] * u
    y_ref[...] = _gelu_tanh(y).astype(BF16).reshape(y_ref.shape)
    ht_ref[...] = state[...]


def _s5(u_tm, h0, a_row, bmat, cmat, d_row, *, tc):
    nt, bs, _ = u_tm.shape
    rb = 16 if bs % 16 == 0 else 8
    assert bs % rb == 0 and nt % tc == 0
    return pl.pallas_call(
        functools.partial(_s5_kernel, tc=tc, bs=bs, rb=rb, lc=256),
        grid=(nt // tc,),
        in_specs=[pl.BlockSpec((tc, bs, S_WIDTH), lambda t: (t, 0, 0)),
                  _const_spec((bs, 2 * N_STATE)), _const_spec((1, 2 * N_STATE)),
                  _const_spec((S_WIDTH, 2 * N_STATE)), _const_spec((2 * N_STATE, S_WIDTH)),
                  _const_spec((1, S_WIDTH))],
        out_specs=[pl.BlockSpec((tc, bs, S_WIDTH), lambda t: (t, 0, 0)),
                   pl.BlockSpec((bs, 2 * N_STATE), lambda t: (0, 0))],
        out_shape=[jax.ShapeDtypeStruct((nt, bs, S_WIDTH), BF16),
                   jax.ShapeDtypeStruct((bs, 2 * N_STATE), F32)],
        scratch_shapes=[pltpu.VMEM((tc * bs, 2 * N_STATE), F32), pltpu.VMEM((bs, 2 * N_STATE), F32)],
        compiler_params=_cparams("arbitrary"),
        name="s5",
    )(u_tm, h0, a_row, bmat, cmat, d_row)


N_POST_IN = 17


def _post_kernel(*refs, bb, tt, ffc, sattn):
    if sattn is None:
        post_in, (y_ref, h2_sc, down_sc) = refs[:N_POST_IN], refs[N_POST_IN:]
    else:
        pt_ref, refs = refs[0], refs[1:]
        post_in, rest = refs[:N_POST_IN], refs[N_POST_IN:]
        sa_in, (y_ref, o_ref), (h2_sc, down_sc), sa_sc = rest[:7], rest[7:9], rest[9:11], rest[11:]
        sa_refs = (pt_ref,) + tuple(sa_in) + (o_ref,) + tuple(sa_sc)
    (x_ref, a_ref, ys_ref, sga_ref, sgs_ref, g1_ref, sh2_ref, sc2_ref, g2_ref, gffn_ref,
     wglu_ref, bglu_ref, wupa_ref, wups_ref, wout_ref, wffu_ref, wffd_ref) = post_in
    rows = bb * tt

    def mix():
        ys = ys_ref[...].reshape(rows, S_WIDTH)
        glu = _dot(ys, wglu_ref[...]) + bglu_ref[...]
        s = (ys.astype(F32) * jax.nn.sigmoid(glu)).astype(BF16)
        a = a_ref[...].reshape(rows, A_WIDTH)
        merged = (sga_ref[...].reshape(rows, D_MODEL).astype(F32) * _dot(a, wupa_ref[...])
                  + sgs_ref[...].reshape(rows, D_MODEL).astype(F32) * _dot(s, wups_ref[...]))
        x1 = x_ref[...] + g1_ref[...] * _dot(merged.astype(BF16), wout_ref[...]).reshape(bb, tt, D_MODEL)
        y_ref[...] = x1
        ms = jnp.mean(x1 * x1, axis=-1, keepdims=True)
        h2 = x1 * lax.rsqrt(ms + NORM_EPS) * gffn_ref[...]
        h2_sc[...] = (h2 * (1.0 + sc2_ref[...]) + sh2_ref[...]).reshape(rows, D_MODEL).astype(BF16)

    def ffn(c):
        up = jnp.maximum(_dot(h2_sc[...], wffu_ref[:, ffc * c:ffc * (c + 1)]), 0.0)
        part = _dot((up * up).astype(BF16), wffd_ref[ffc * c:ffc * (c + 1), :])
        if c == 0:
            down_sc[...] = part
        else:
            down_sc[...] += part

    work = [mix] + [functools.partial(ffn, c) for c in range(D_FF // ffc)]
    if sattn is None:
        for item in work:
            item()
    else:
        g = pl.program_id(0) * pl.num_programs(1) + pl.program_id(1)
        cps, done = sattn["cps"], 0
        for idx in range(cps):
            _sattn_chunk(g, idx, sa_refs, **sattn)
            upto = ((idx + 1) * len(work) + cps // 2) // cps
            for item in work[done:upto]:
                item()
            done = max(done, upto)
        assert done == len(work)
    y_ref[...] = y_ref[...] + g2_ref[...] * down_sc[...].reshape(bb, tt, D_MODEL)


def _post(x, a, ys, sga, sgs, g1, sh2, sc2, g2, gffn, wglu, bglu, wupa, wups, wout, wffu, wffd,
          *, bb, tt, ys_time_major, sattn_args=None, pp=8):
    nb, nt_total, _ = x.shape
    grid = (nb // bb, nt_total // tt)
    fused = sattn_args is not None

    def imap(f):
        return (lambda b, t, pt: f(b, t)) if fused else f

    def tok(width):
        return pl.BlockSpec((bb, tt, width), imap(lambda b, t: (b, t, 0)))

    def mod():
        return pl.BlockSpec((bb, 1, D_MODEL), imap(lambda b, t: (b, 0, 0)))

    def const(shape):
        nd = len(shape)
        return pl.BlockSpec(shape, imap(lambda b, t: (0,) * nd), pipeline_mode=pl.Buffered(1))

    if ys_time_major:
        assert bb == 1
        ys_spec = pl.BlockSpec((tt, S_WIDTH), imap(lambda b, t: (t, b)))
    else:
        ys_spec = tok(S_WIDTH)

    in_specs = [tok(D_MODEL), tok(A_WIDTH), ys_spec, tok(D_MODEL), tok(D_MODEL),
                mod(), mod(), mod(), mod(), const((1, D_MODEL)),
                const((S_WIDTH, S_WIDTH)), const((1, S_WIDTH)),
                const((A_WIDTH, D_MODEL)), const((S_WIDTH, D_MODEL)),
                const((D_MODEL, D_MODEL)), const((D_MODEL, D_FF)), const((D_FF, D_MODEL))]
    assert len(in_specs) == N_POST_IN
    operands = [x, a, ys, sga, sgs, g1, sh2, sc2, g2, gffn, wglu, bglu, wupa, wups, wout, wffu, wffd]
    out_specs = tok(D_MODEL)
    out_shape = jax.ShapeDtypeStruct((nb, nt_total, D_MODEL), F32)
    rows = bb * tt
    scratch = [pltpu.VMEM((rows, D_MODEL), BF16), pltpu.VMEM((rows, D_MODEL), F32)]
    static = None
    semantics = ("parallel", "parallel")
    if fused:
        page_table, qbd, qabd, kan, vn, cache_kt, cache_vt, rsuf = sattn_args
        nbs, n_pages = page_table.shape
        ts = kan.shape[1]
        nrow = N_HEADS * ts
        n_steps = grid[0] * grid[1]
        assert nbs % n_steps == 0 and n_pages % pp == 0
        spp = nbs // n_steps
        cps = spp * (n_pages // pp)
        assert cps % 2 == 0
        static = dict(pp=pp, ts=ts, n_pages=n_pages, cps=cps)

        def per_step(shape):
            return pl.BlockSpec(shape, lambda b, t, pt: (b * grid[1] + t, 0, 0))

        hbm = pl.BlockSpec(memory_space=pl.ANY)
        in_specs += [per_step((spp, nrow, A_WIDTH)), per_step((spp, nrow, QA_WIDTH)),
                     per_step((spp, ts, QA_WIDTH)), per_step((spp, ts, A_WIDTH)), hbm, hbm, hbm]
        operands = [page_table.reshape(-1)] + operands + [qbd, qabd, kan, vn, cache_kt, cache_vt, rsuf]
        out_specs = [out_specs, per_step((spp, ts, A_WIDTH))]
        out_shape = [out_shape, jax.ShapeDtypeStruct((nbs, ts, A_WIDTH), BF16)]
        scratch += [pltpu.VMEM((2, pp, A_WIDTH, PAGE), F32), pltpu.VMEM((2, pp, A_WIDTH, PAGE), F32),
                    pltpu.VMEM((2, pp, N_HEADS, 2 * PAGE), F32), pltpu.SemaphoreType.DMA((3, 2)),
                    pltpu.VMEM((A_WIDTH, pp * PAGE), BF16), pltpu.VMEM((A_WIDTH, pp * PAGE), BF16),
                    pltpu.VMEM((nrow, 1), F32), pltpu.VMEM((nrow, 1), F32),
                    pltpu.VMEM((nrow, A_WIDTH), F32), pltpu.VMEM((N_HEADS, PAGE), F32)]
        semantics = ("arbitrary", "arbitrary")

    grid_spec = pltpu.PrefetchScalarGridSpec(
        num_scalar_prefetch=1 if fused else 0, grid=grid, in_specs=in_specs, out_specs=out_specs,
        scratch_shapes=scratch)
    return pl.pallas_call(
        functools.partial(_post_kernel, bb=bb, tt=tt, ffc=512, sattn=static),
        grid_spec=grid_spec,
        out_shape=out_shape,
        compiler_params=pltpu.CompilerParams(dimension_semantics=semantics, vmem_limit_bytes=POST_VMEM_LIMIT),
        name="post_sattn" if fused else "post",
    )(*operands)


def _block_diag_heads(q, width):
    nb, ts, _ = q.shape
    q5 = q.reshape(nb, ts, 1, N_HEADS, width)
    keep = jnp.eye(N_HEADS, dtype=bool).reshape(1, 1, N_HEADS, N_HEADS, 1)
    return jnp.where(keep, q5, jnp.zeros((), q.dtype)).reshape(nb, ts * N_HEADS, N_HEADS * width)


def kernel(x_prompt, x_sample, cache_k, cache_v, cache_logf, state_ssm_re, state_ssm_im, page_table,
           c_prompt, c_sample, w_ada, b_ada, norm_mix_g, norm_ffn_g, w_in, b_fgate, q_norm_g, k_norm_g,
           ssm_lambda_re, ssm_lambda_im, ssm_log_dt, ssm_b_re, ssm_b_im, ssm_c_re, ssm_c_im, ssm_d,
           w_glu, b_glu, w_up_a, w_up_s, w_out, w_ffn_up, w_ffn_down):
    nbp, ntp, _ = x_prompt.shape
    nbs, nts, _ = x_sample.shape
    n_pool = cache_k.shape[0]

    a3 = 3 * A_WIDTH
    wf_pad = jnp.pad(w_in[:, a3:a3 + N_HEADS], ((0, 0), (0, LANES - N_HEADS)))
    w2 = jnp.concatenate([w_in[:, :a3], w_in[:, a3 + N_HEADS:], wf_pad], axis=1).astype(BF16)
    bfp = jnp.pad(b_fgate, (0, LANES - N_HEADS)).reshape(1, LANES)
    wt = jnp.concatenate([w_in[:, :A_WIDTH].T, w_in[:, 2 * A_WIDTH:a3].T], axis=0).astype(BF16)
    gqp = jnp.pad(q_norm_g, (0, LANES - D_HEAD)).reshape(1, LANES)
    gq_col = q_norm_g.reshape(D_HEAD, 1)
    gkp = jnp.pad(k_norm_g, (0, LANES - D_HEAD)).reshape(1, LANES)
    heads = jnp.arange(N_HEADS)
    e2k = jnp.zeros((LANES, QA_WIDTH), F32)
    cq = jnp.zeros((1, QA_WIDTH), F32)
    for piece in range(N_SPLIT):
        e2k = e2k.at[piece * N_HEADS + heads, HEAD_PAD * heads + D_HEAD + piece].set(-1.0)
        cq = cq.at[0, HEAD_PAD * heads + D_HEAD + piece].set(1.0)
    e2k = e2k.astype(BF16)
    gmix = norm_mix_g.reshape(1, D_MODEL)
    gffn = norm_ffn_g.reshape(1, D_MODEL)
    post_w = (w_glu.astype(BF16), b_glu.reshape(1, S_WIDTH), w_up_a.astype(BF16), w_up_s.astype(BF16),
              w_out.astype(BF16), w_ffn_up.astype(BF16), w_ffn_down.astype(BF16))

    mod = _adaln(jnp.concatenate([c_prompt, c_sample], axis=0), w_ada.astype(BF16), b_ada)
    mod = mod.reshape(nbp + nbs, 1, 6, D_MODEL)
    mods_p = [mod[:nbp, :, i, :] for i in range(6)]
    mods_s = [mod[nbp:, :, i, :] for i in range(6)]

    ab_re, ab_im, bb_re, bb_im = _s5_disc(ssm_lambda_re, ssm_lambda_im, ssm_log_dt, ssm_b_re, ssm_b_im)
    a_row = jnp.concatenate([ab_re.reshape(1, N_STATE), ab_im.reshape(1, N_STATE)], axis=1)
    eye_g = jnp.eye(N_GROUPS, dtype=F32)

    def b_blockdiag(bb):
        return jnp.einsum('gcp,gh->gchp', bb, eye_g).reshape(S_WIDTH, N_STATE)

    def c_blockdiag(cc):
        return jnp.einsum('gcp,gh->gphc', cc, eye_g).reshape(N_STATE, S_WIDTH)

    bmat = jnp.concatenate([b_blockdiag(bb_re), b_blockdiag(bb_im)], axis=1).astype(BF16)
    cmat = jnp.concatenate([c_blockdiag(ssm_c_re), -c_blockdiag(ssm_c_im)], axis=0).astype(BF16)
    d_row = ssm_d.reshape(1, S_WIDTH)

    tt = min(512, ntp)
    ablk = min(512, ntp)
    bbs = min(64, nbs)
    qa_s, ka_s, k_s, v_s, vb_s, lf_s, u_s, sga_s, sgs_s, qn_s = _inproj(
        x_sample, mods_s[0], mods_s[1], gmix, w2, wt, bfp, gqp, gq_col, gkp, e2k, cq,
        bb=bbs, tt=nts, prompt=False, kblk=ablk)
    lf_rows = jnp.swapaxes(cache_logf, 1, 2).reshape(n_pool * N_HEADS, PAGE)
    rsuf = _pool_logf(lf_rows).reshape(n_pool, N_HEADS, 2 * PAGE)
    cache_kt = jnp.transpose(cache_k, (0, 2, 3, 1)).reshape(n_pool, A_WIDTH, PAGE)
    cache_vt = jnp.transpose(cache_v, (0, 2, 3, 1)).reshape(n_pool, A_WIDTH, PAGE)
    sattn_args = (page_table, _block_diag_heads(qn_s, D_HEAD), _block_diag_heads(qa_s, HEAD_PAD),
                  ka_s, vb_s, cache_kt, cache_vt, rsuf)

    qat, ka, k_p, vt_p, vtb, lf_p, u_tm, sga, sgs = _inproj(
        x_prompt, mods_p[0], mods_p[1], gmix, w2, wt, bfp, gqp, gq_col, gkp, e2k, cq,
        bb=1, tt=tt, prompt=True, kblk=ablk)
    attn_p = _attn(qat, ka, vtb, qblk=min(512, ntp), kblk=ablk)
    v_p = jnp.transpose(vt_p.reshape(nbp, N_HEADS, D_HEAD, ntp), (0, 3, 1, 2))
    h0 = jnp.zeros((nbp, 2 * N_STATE), F32)
    ys_tm, ht_p = _s5(u_tm.reshape(ntp, nbp, S_WIDTH), h0, a_row, bmat, cmat, d_row, tc=min(32, ntp))
    y_prompt, attn_s = _post(x_prompt, attn_p, ys_tm.reshape(ntp, nbp * S_WIDTH), sga, sgs,
                             mods_p[2], mods_p[3], mods_p[4], mods_p[5], gffn, *post_w,
                             bb=1, tt=tt, ys_time_major=True, sattn_args=sattn_args)

    h0_s = jnp.concatenate([state_ssm_re.reshape(nbs, N_STATE), state_ssm_im.reshape(nbs, N_STATE)], axis=1)
    ys_s_tm, ht_s = _s5(jnp.swapaxes(u_s, 0, 1), h0_s, a_row, bmat, cmat, d_row, tc=nts)
    y_sample = _post(x_sample, attn_s, jnp.swapaxes(ys_s_tm, 0, 1), sga_s, sgs_s,
                     mods_s[2], mods_s[3], mods_s[4], mods_s[5], gffn, *post_w,
                     bb=bbs, tt=nts, ys_time_major=False)

    def heads4(z):
        return z.reshape(z.shape[0], z.shape[1], N_HEADS, D_HEAD)

    def state3(z):
        return z.reshape(z.shape[0], N_GROUPS, STATE_P)

    return (y_prompt, y_sample, heads4(k_p), heads4(v_p), lf_p,
            state3(ht_p[:, :N_STATE]), state3(ht_p[:, N_STATE:]),
            heads4(k_s), heads4(v_s), lf_s,
            state3(ht_s[:, :N_STATE]), state3(ht_s[:, N_STATE:]))
```

```python
import functools
import math

import jax
import jax.numpy as jnp
from jax import lax
from jax.experimental import pallas as pl
from jax.experimental.pallas import tpu as pltpu

F32 = jnp.float32
BF16 = jnp.bfloat16

D_MODEL = 1024
N_HEADS = 8
D_HEAD = 64
A_WIDTH = N_HEADS * D_HEAD
S_WIDTH = D_MODEL // 2
GROUP_CH = 16
N_GROUPS = S_WIDTH // GROUP_CH
STATE_P = 64
N_STATE = N_GROUPS * STATE_P
D_FF = 4 * D_MODEL
PAGE = 128
ATTN_SCALE = 1.0 / math.sqrt(D_HEAD)
LOG2_E = math.log2(math.e)
NORM_EPS = 1e-6
NEG_BIG = -1e30

LANES = 128
MXU_TILE = 256
HEAD_PAD = 128
QA_WIDTH = N_HEADS * HEAD_PAD
N_SPLIT = 3
VMEM_LIMIT = 56 * 1024 * 1024
POST_VMEM_LIMIT = 60 * 1024 * 1024
PAGE_SLOTS = 3

_OQ, _OK, _OV, _OU, _OGA, _OGS, _OF, _W2 = 0, 512, 1024, 1536, 2048, 3072, 4096, 4224


def _cparams(*sem):
    return pltpu.CompilerParams(dimension_semantics=sem, vmem_limit_bytes=VMEM_LIMIT)


def _const_spec(shape):
    nd = len(shape)
    return pl.BlockSpec(shape, lambda *_: (0,) * nd, pipeline_mode=pl.Buffered(1))


def _split3(x):
    hi = x.astype(BF16)
    r1 = x - hi.astype(F32)
    mid = r1.astype(BF16)
    lo = (r1 - mid.astype(F32)).astype(BF16)
    return hi, mid, lo


def _dot(a, b):
    return jnp.dot(a, b, preferred_element_type=F32)


def _dot_nt(a, b):
    return lax.dot_general(a, b, (((1,), (1,)), ((), ())), preferred_element_type=F32)


def _adaln_kernel(c_ref, w_ref, b_ref, o_ref):
    c = c_ref[...]
    s = (c * jax.nn.sigmoid(c)).astype(BF16)
    o_ref[...] = _dot(s, w_ref[...]) + b_ref[...]


def _adaln(c_all, w_ada_bf, b_ada):
    n = c_all.shape[0]
    tn = 1024
    return pl.pallas_call(
        _adaln_kernel,
        grid=(6 * D_MODEL // tn,),
        in_specs=[pl.BlockSpec((n, D_MODEL), lambda j: (0, 0)),
                  pl.BlockSpec((D_MODEL, tn), lambda j: (0, j)),
                  pl.BlockSpec((1, tn), lambda j: (0, j))],
        out_specs=pl.BlockSpec((n, tn), lambda j: (0, j)),
        out_shape=jax.ShapeDtypeStruct((n, 6 * D_MODEL), F32),
        compiler_params=_cparams("parallel"),
        name="adaln",
    )(c_all, w_ada_bf, b_ada.reshape(1, -1))


def _s5_disc_kernel(lr_ref, li_ref, ldt_ref, br_ref, bi_ref, abr_ref, abi_ref, bbr_ref, bbi_ref):
    lr = lr_ref[...]
    li = li_ref[...]
    dt = jnp.exp(ldt_ref[...])
    mag = jnp.exp(lr * dt)
    ab_re = mag * jnp.cos(li * dt)
    ab_im = mag * jnp.sin(li * dt)
    nr, ni = ab_re - 1.0, ab_im
    den = lr * lr + li * li
    f_re = (nr * lr + ni * li) / den
    f_im = (ni * lr - nr * li) / den
    br = br_ref[...]
    bi = bi_ref[...]
    abr_ref[...] = ab_re
    abi_ref[...] = ab_im
    bbr_ref[...] = f_re * br - f_im * bi
    bbi_ref[...] = f_re * bi + f_im * br


def _s5_disc(lam_re, lam_im, log_dt, b_re, b_im):
    g3 = jax.ShapeDtypeStruct((N_GROUPS, 1, STATE_P), F32)
    b3 = jax.ShapeDtypeStruct((N_GROUPS, GROUP_CH, STATE_P), F32)
    return pl.pallas_call(
        _s5_disc_kernel,
        out_shape=(g3, g3, b3, b3),
        name="s5_disc",
    )(lam_re.reshape(N_GROUPS, 1, STATE_P), lam_im.reshape(N_GROUPS, 1, STATE_P),
      log_dt.reshape(N_GROUPS, 1, 1),
      jnp.swapaxes(b_re, 1, 2), jnp.swapaxes(b_im, 1, 2))


def _inproj_kernel(x_ref, sh_ref, sc_ref, gmix_ref, w_ref, wt_ref, bf_ref, gq_ref, gqcol_ref, gk_ref,
                   e2k_ref, cq_ref, tri_ref, *rest, bb, tt, prompt, kblk):
    if prompt:
        qat_ref, ka_ref, k_ref, vt_ref, vtb_ref, lf_ref, u_ref, sga_ref, sgs_ref, carry_ref = rest
    else:
        qa_ref, ka_ref, k_ref, v_ref, vb_ref, lf_ref, u_ref, sga_ref, sgs_ref, qn_ref, carry_ref = rest
    rows = bb * tt
    t = pl.program_id(1)

    x = x_ref[...]
    ms = jnp.mean(x * x, axis=-1, keepdims=True)
    h = x * lax.rsqrt(ms + NORM_EPS) * gmix_ref[...]
    h = h * (1.0 + sc_ref[...]) + sh_ref[...]
    hb = h.reshape(rows, D_MODEL).astype(BF16)

    lane = lax.broadcasted_iota(jnp.int32, (1, LANES), 1)
    low_half = lane < D_HEAD

    zf = _dot(hb, w_ref[:, _OF:_W2]) + bf_ref[...]
    lf = jnp.minimum(zf, 0.0) - jnp.log1p(jnp.exp(-jnp.abs(zf)))
    lf = jnp.where(lane < N_HEADS, lf, 0.0)
    lf_ref[...] = lf[:, :N_HEADS].reshape(lf_ref.shape)
    hi, mid, lo = _split3(lf)
    packed = (hi.astype(F32) + pltpu.roll(mid.astype(F32), N_HEADS, 1)
              + pltpu.roll(lo.astype(F32), 2 * N_HEADS, 1)).astype(BF16)
    gc = _dot(tri_ref[...], packed)
    g = gc + pltpu.roll(gc, LANES - N_HEADS, 1) + pltpu.roll(gc, LANES - 2 * N_HEADS, 1)
    g = jnp.where(lane < N_HEADS, g, 0.0)
    if bb == 1:
        @pl.when(t == 0)
        def _():
            carry_ref[...] = jnp.zeros_like(carry_ref)
        g = g + carry_ref[...]
        carry_ref[...] = g[rows - 1:rows, :]
    logit_scale = LOG2_E if prompt else 1.0
    ghi, gmid, glo = _split3(g * logit_scale)
    gpacked = (ghi.astype(F32) + pltpu.roll(gmid.astype(F32), N_HEADS, 1)
               + pltpu.roll(glo.astype(F32), 2 * N_HEADS, 1)).astype(BF16)
    kaug = _dot(gpacked, e2k_ref[...])

    def widen_norm(z, gain):
        out = []
        for part in (z, pltpu.roll(z, D_HEAD, 1)):
            e = jnp.where(low_half, part, 0.0)
            ss = jnp.sum(e * e, axis=-1, keepdims=True)
            out.append(e * lax.rsqrt(ss * (1.0 / D_HEAD) + NORM_EPS) * gain)
        return out

    gq = gq_ref[...] * ATTN_SCALE
    gk = gk_ref[...]
    for j in range(N_HEADS // 2):
        zk = _dot(hb, w_ref[:, _OK + LANES * j:_OK + LANES * (j + 1)])
        k_even, k_odd = widen_norm(zk, gk)
        for i, kh in enumerate((k_even, k_odd)):
            c0 = HEAD_PAD * (2 * j + i)
            ka_ref[..., c0:c0 + HEAD_PAD] = (kh + kaug[:, c0:c0 + HEAD_PAD]).astype(BF16).reshape(bb, tt, HEAD_PAD)
        k_ref[..., LANES * j:LANES * (j + 1)] = (k_even + pltpu.roll(k_odd, D_HEAD, 1)).reshape(bb, tt, LANES)
        if not prompt:
            zq = _dot(hb, w_ref[:, _OQ + LANES * j:_OQ + LANES * (j + 1)])
            q_even, q_odd = widen_norm(zq, gq)
            for i, qh in enumerate((q_even, q_odd)):
                c0 = HEAD_PAD * (2 * j + i)
                qa_ref[..., c0:c0 + HEAD_PAD] = (
                    (qh + cq_ref[:, c0:c0 + HEAD_PAD]).astype(BF16).reshape(bb, tt, HEAD_PAD))
            qn_ref[..., LANES * j:LANES * (j + 1)] = (
                (q_even + pltpu.roll(q_odd, D_HEAD, 1)).astype(BF16).reshape(bb, tt, LANES))

    if prompt:
        zqt = _dot_nt(wt_ref[:A_WIDTH, :], hb)
        gq_col = gqcol_ref[...] * (ATTN_SCALE * logit_scale)
        sub = lax.broadcasted_iota(jnp.int32, (D_HEAD, tt), 0)
        ones_rows = jnp.where(sub < N_SPLIT, 1.0, 0.0).astype(BF16)
        for hd in range(N_HEADS):
            z = zqt[D_HEAD * hd:D_HEAD * (hd + 1), :]
            ss = jnp.sum(z * z, axis=0, keepdims=True)
            qn_t = z * lax.rsqrt(ss * (1.0 / D_HEAD) + NORM_EPS) * gq_col
            qat_ref[0, HEAD_PAD * hd:HEAD_PAD * hd + D_HEAD, :] = qn_t.astype(BF16)
            qat_ref[0, HEAD_PAD * hd + D_HEAD:HEAD_PAD * (hd + 1), :] = ones_rows
        zvt = _dot_nt(wt_ref[A_WIDTH:, :], hb)
        vt_ref[0] = zvt
        for c in range(tt // kblk):
            vtb_ref[0, c] = zvt[:, kblk * c:kblk * (c + 1)].astype(BF16)
    else:
        zv = _dot(hb, w_ref[:, _OV:_OU])
        v_ref[...] = zv.reshape(v_ref.shape)
        vb_ref[...] = zv.astype(BF16).reshape(vb_ref.shape)
    u_ref[...] = _dot(hb, w_ref[:, _OU:_OGA]).reshape(u_ref.shape)
    sga_ref[...] = jax.nn.sigmoid(_dot(hb, w_ref[:, _OGA:_OGS])).astype(BF16).reshape(sga_ref.shape)
    sgs_ref[...] = jax.nn.sigmoid(_dot(hb, w_ref[:, _OGS:_OF])).astype(BF16).reshape(sgs_ref.shape)


def _inproj(x, sh1, sc1, gmix, w2, wt, bfp, gqp, gq_col, gkp, e2k, cq, *, bb, tt, prompt, kblk):
    nb, nt_total, _ = x.shape
    u_time_major = prompt
    rows = bb * tt
    idx = jnp.arange(rows)
    tri = ((idx[:, None] >= idx[None, :]) & (idx[:, None] // tt == idx[None, :] // tt)).astype(BF16)
    grid = (nb // bb, nt_total // tt)
    assert bb == 1 or grid[1] == 1

    def tok(width):
        return pl.BlockSpec((bb, tt, width), lambda b, t: (b, t, 0))

    def mod():
        return pl.BlockSpec((bb, 1, D_MODEL), lambda b, t: (b, 0, 0))

    if u_time_major:
        assert bb == 1
        u_shape = jax.ShapeDtypeStruct((nt_total, nb * S_WIDTH), F32)
        u_spec = pl.BlockSpec((tt, S_WIDTH), lambda b, t: (t, b))
    else:
        u_shape = jax.ShapeDtypeStruct((nb, nt_total, S_WIDTH), F32)
        u_spec = tok(S_WIDTH)

    def sds(width, dt):
        return jax.ShapeDtypeStruct((nb, nt_total, width), dt)

    tail_shape = [sds(N_HEADS, F32), u_shape, sds(D_MODEL, BF16), sds(D_MODEL, BF16)]
    tail_specs = [tok(N_HEADS), u_spec, tok(D_MODEL), tok(D_MODEL)]
    if prompt:
        assert tt % kblk == 0
        out_shape = [jax.ShapeDtypeStruct((nb, QA_WIDTH, nt_total), BF16), sds(QA_WIDTH, BF16), sds(A_WIDTH, F32),
                     jax.ShapeDtypeStruct((nb, A_WIDTH, nt_total), F32),
                     jax.ShapeDtypeStruct((nb, nt_total // kblk, A_WIDTH, kblk), BF16)] + tail_shape
        out_specs = [pl.BlockSpec((1, QA_WIDTH, tt), lambda b, t: (b, 0, t)), tok(QA_WIDTH), tok(A_WIDTH),
                     pl.BlockSpec((1, A_WIDTH, tt), lambda b, t: (b, 0, t)),
                     pl.BlockSpec((1, tt // kblk, A_WIDTH, kblk), lambda b, t: (b, t, 0, 0))] + tail_specs
    else:
        out_shape = ([sds(QA_WIDTH, BF16), sds(QA_WIDTH, BF16), sds(A_WIDTH, F32), sds(A_WIDTH, F32),
                      sds(A_WIDTH, BF16)] + tail_shape + [sds(A_WIDTH, BF16)])
        out_specs = ([tok(QA_WIDTH), tok(QA_WIDTH), tok(A_WIDTH), tok(A_WIDTH), tok(A_WIDTH)]
                     + tail_specs + [tok(A_WIDTH)])

    return pl.pallas_call(
        functools.partial(_inproj_kernel, bb=bb, tt=tt, prompt=prompt, kblk=kblk),
        grid=grid,
        in_specs=[tok(D_MODEL), mod(), mod(), _const_spec((1, D_MODEL)), _const_spec((D_MODEL, _W2)),
                  _const_spec((2 * A_WIDTH, D_MODEL)),
                  _const_spec((1, LANES)), _const_spec((1, LANES)), _const_spec((D_HEAD, 1)), _const_spec((1, LANES)),
                  _const_spec((LANES, QA_WIDTH)), _const_spec((1, QA_WIDTH)), _const_spec((rows, rows))],
        out_specs=out_specs,
        out_shape=out_shape,
        scratch_shapes=[pltpu.VMEM((1, LANES), F32)],
        compiler_params=_cparams("parallel", "arbitrary"),
        name="inproj",
    )(x, sh1, sc1, gmix, w2, wt, bfp, gqp, gq_col, gkp, e2k, cq, tri)


def _attn_kernel(qat_ref, ka_ref, vtb_ref, o_ref, m_sc, l_sc, acc_sc, *, qblk, kblk):
    qi = pl.program_id(1)
    ratio = qblk // kblk
    m_sc[...] = jnp.full_like(m_sc, NEG_BIG)
    l_sc[...] = jnp.zeros_like(l_sc)
    acc_sc[...] = jnp.zeros_like(acc_sc)
    key = lax.broadcasted_iota(jnp.int32, (kblk, qblk), 0)
    qry = lax.broadcasted_iota(jnp.int32, (kblk, qblk), 1)

    def block(kb, diag):
        ks = pl.multiple_of(kb * kblk, kblk)
        for h in range(N_HEADS):
            k = ka_ref[0, pl.ds(ks, kblk), HEAD_PAD * h:HEAD_PAD * (h + 1)]
            st = _dot(k, qat_ref[0, HEAD_PAD * h:HEAD_PAD * (h + 1), :])
            if diag is not None:
                st = jnp.where(key + diag * kblk <= qry, st, NEG_BIG)
            m_old = m_sc[h]
            m_new = jnp.maximum(m_old, jnp.max(st, axis=0, keepdims=True))
            alpha = jnp.exp2(m_old - m_new)
            p = jnp.exp2(st - m_new)
            l_sc[h] = alpha * l_sc[h] + jnp.sum(p, axis=0, keepdims=True)
            vt = vtb_ref[0, kb, D_HEAD * h:D_HEAD * (h + 1), :]
            rows = slice(D_HEAD * h, D_HEAD * (h + 1))
            acc_sc[rows, :] = alpha * acc_sc[rows, :] + _dot(vt, p.astype(BF16))
            m_sc[h] = m_new

    def body(kb, carry):
        block(kb, None)
        return carry

    lax.fori_loop(0, qi * ratio, body, 0)
    for d in range(ratio):
        block(qi * ratio + d, d)

    for hp in range(N_HEADS // 2):
        inv = jnp.concatenate([jnp.broadcast_to(1.0 / l_sc[2 * hp + i], (D_HEAD, qblk)) for i in range(2)], axis=0)
        pair = acc_sc[LANES * hp:LANES * (hp + 1), :] * inv
        o_ref[0, :, LANES * hp:LANES * (hp + 1)] = pair.T.astype(BF16)


def _attn(qat, ka, vtb, *, qblk, kblk):
    nb, nt, _ = ka.shape
    assert vtb.shape == (nb, nt // kblk, A_WIDTH, kblk) and qblk % kblk == 0
    return pl.pallas_call(
        functools.partial(_attn_kernel, qblk=qblk, kblk=kblk),
        grid=(nb, nt // qblk),
        in_specs=[pl.BlockSpec((1, QA_WIDTH, qblk), lambda b, i: (b, 0, i)),
                  pl.BlockSpec((1, nt, QA_WIDTH), lambda b, i: (b, 0, 0)),
                  pl.BlockSpec((1, nt // kblk, A_WIDTH, kblk), lambda b, i: (b, 0, 0, 0))],
        out_specs=pl.BlockSpec((1, qblk, A_WIDTH), lambda b, i: (b, i, 0)),
        out_shape=jax.ShapeDtypeStruct((nb, nt, A_WIDTH), BF16),
        scratch_shapes=[pltpu.VMEM((N_HEADS, 1, qblk), F32), pltpu.VMEM((N_HEADS, 1, qblk), F32),
                        pltpu.VMEM((A_WIDTH, qblk), F32)],
        compiler_params=_cparams("parallel", "arbitrary"),
        name="attn",
    )(qat, ka, vtb)


def _pool_logf_kernel(lf_ref, m_ref, o_ref):
    hi, mid, lo = _split3(lf_ref[...])
    m = m_ref[...]
    o_ref[...] = _dot(hi, m) + _dot(mid, m) + _dot(lo, m)


def _pool_logf(lf_rows):
    n_rows = lf_rows.shape[0]
    tile = next(c for c in (2048, 1024, 512, 256, 128, 64, 32, 16, 8) if n_rows % c == 0)
    j_src = jnp.arange(PAGE)
    suffix = j_src[:, None] > j_src[None, :]
    m = jnp.concatenate([suffix, jnp.ones((PAGE, PAGE), bool)], axis=1).astype(BF16)
    return pl.pallas_call(
        _pool_logf_kernel,
        grid=(n_rows // tile,),
        in_specs=[pl.BlockSpec((tile, PAGE), lambda i: (i, 0)), _const_spec((PAGE, 2 * PAGE))],
        out_specs=pl.BlockSpec((tile, 2 * PAGE), lambda i: (i, 0)),
        out_shape=jax.ShapeDtypeStruct((n_rows, 2 * PAGE), F32),
        compiler_params=_cparams("parallel"),
        name="pool_logf",
    )(lf_rows, m)


def _sattn_chunk(g, idx, refs, *, pp, ts, n_pages, cps):
    (pt_ref, qbd_ref, qabd_ref, kan_ref, vn_ref, k_hbm, v_hbm, r_hbm, o_ref,
     kpage, vpage, rpage, sem, m_sc, l_sc, acc_sc, tot_sc) = refs
    cpq = n_pages // pp
    spp = cps // cpq
    n_steps = pl.num_programs(0) * pl.num_programs(1)
    nrow = N_HEADS * ts

    def page_copies(ahead):
        g_, idx_ = g + (idx + ahead) // cps, (idx + ahead) % cps
        seq, chunk = g_ * spp + idx_ // cpq, idx_ % cpq
        slot_ = lax.rem(g_ * cps + idx_, PAGE_SLOTS)
        out = []
        for i in range(pp):
            pid = pt_ref[seq * n_pages + (n_pages - 1 - (chunk * pp + i))]
            lanes = pl.ds(PAGE * i, PAGE)
            out.append(pltpu.make_async_copy(k_hbm.at[pid], kpage.at[slot_, :, lanes], sem.at[0, slot_]))
            out.append(pltpu.make_async_copy(v_hbm.at[pid], vpage.at[slot_, :, lanes], sem.at[1, slot_]))
            out.append(pltpu.make_async_copy(r_hbm.at[pid], rpage.at[slot_, i], sem.at[2, slot_]))
        return out

    depth = PAGE_SLOTS - 1
    if idx == 0:
        @pl.when(g == 0)
        def _():
            for ahead in range(depth):
                for cp in page_copies(ahead):
                    cp.start()
    if idx + depth < cps:
        for cp in page_copies(depth):
            cp.start()
    else:
        @pl.when(g + 1 < n_steps)
        def _():
            for cp in page_copies(depth):
                cp.start()
    for cp in page_copies(0):
        cp.wait()

    slot = lax.rem(g * cps + idx, PAGE_SLOTS)
    seq_l, chunk = idx // cpq, idx % cpq
    if chunk == 0:
        m_sc[...] = jnp.full_like(m_sc, NEG_BIG)
        l_sc[...] = jnp.zeros_like(l_sc)
        acc_sc[...] = jnp.zeros_like(acc_sc)
        tot_sc[...] = jnp.zeros_like(tot_sc)

    def update(s, pv):
        m_old = m_sc[...]
        m_new = jnp.maximum(m_old, jnp.max(s, axis=-1, keepdims=True))
        alpha = jnp.exp(m_old - m_new)
        p = jnp.exp(s - m_new)
        l_sc[...] = alpha * l_sc[...] + jnp.sum(p, axis=-1, keepdims=True)
        acc_sc[...] = alpha * acc_sc[...] + pv(p)
        m_sc[...] = m_new

    later = tot_sc[...]
    biases = []
    for i in range(pp):
        r = rpage[slot, i]
        biases.append(jnp.concatenate([r[:, :PAGE] + later] * ts, axis=0))
        later = later + r[:, PAGE:]
    tot_sc[...] = later
    s = _dot(qbd_ref[seq_l].astype(F32), kpage[slot]) + jnp.concatenate(biases, axis=1)
    update(s, lambda p: _dot_nt(p, vpage[slot]))

    if chunk == cpq - 1:
        s_new = _dot_nt(qabd_ref[seq_l], kan_ref[seq_l])
        r_i = lax.broadcasted_iota(jnp.int32, (nrow, ts), 0)
        c_i = lax.broadcasted_iota(jnp.int32, (nrow, ts), 1)
        update(jnp.where((r_i // N_HEADS) >= c_i, s_new, NEG_BIG), lambda p: _dot(p.astype(BF16), vn_ref[seq_l]))
        o = acc_sc[...] / l_sc[...]
        lane_head = lax.broadcasted_iota(jnp.int32, (N_HEADS, A_WIDTH), 1) // D_HEAD
        own = lane_head == lax.broadcasted_iota(jnp.int32, (N_HEADS, A_WIDTH), 0)
        rows = [jnp.sum(jnp.where(own, o[N_HEADS * t:N_HEADS * (t + 1), :], 0.0), axis=0, keepdims=True)
                for t in range(ts)]
        o_ref[seq_l] = jnp.concatenate(rows, axis=0).astype(BF16)


def _gelu_tanh(x):
    return 0.5 * x * (1.0 + jnp.tanh(math.sqrt(2.0 / math.pi) * (x + 0.044715 * (x * x * x))))


def _s5_kernel(u_ref, h0_ref, a_ref, bmat_ref, cmat_ref, d_ref, y_ref, ht_ref, hist, state,
               *, tc, bs, rb, lc):
    t = pl.program_id(0)

    @pl.when(t == 0)
    def _():
        state[...] = h0_ref[...]

    u = u_ref[...].reshape(tc * bs, S_WIDTH)
    ub = u.astype(BF16)
    for n in range(2 * N_STATE // MXU_TILE):
        slab = LANES * ((n % (N_STATE // MXU_TILE)) // 2)
        cols = slice(MXU_TILE * n, MXU_TILE * (n + 1))
        hist[:, cols] = _dot(ub[:, slab:slab + LANES], bmat_ref[slab:slab + LANES, cols])

    for c in range(N_STATE // lc):
        re = slice(lc * c, lc * (c + 1))
        im = slice(N_STATE + lc * c, N_STATE + lc * (c + 1))
        a_re = jnp.broadcast_to(a_ref[:, re], (rb, lc))
        a_im = jnp.broadcast_to(a_ref[:, im], (rb, lc))

        def row_block(r, _, re=re, im=im, a_re=a_re, a_im=a_im):
            r0 = pl.multiple_of(r * rb, rb)

            def step(j, carry):
                h_re, h_im = carry
                row = pl.multiple_of(j * bs + r0, rb)
                n_re = a_re * h_re - a_im * h_im + hist[pl.ds(row, rb), re]
                n_im = a_re * h_im + a_im * h_re + hist[pl.ds(row, rb), im]
                hist[pl.ds(row, rb), re] = n_re
                hist[pl.ds(row, rb), im] = n_im
                return n_re, n_im

            h_re, h_im = lax.fori_loop(0, tc, step, (state[pl.ds(r0, rb), re], state[pl.ds(r0, rb), im]))
            state[pl.ds(r0, rb), re] = h_re
            state[pl.ds(r0, rb), im] = h_im
            return 0

        lax.fori_loop(0, bs // rb, row_block, 0)

    half = N_STATE // (S_WIDTH // MXU_TILE)
    ys = []
    for m_ in range(S_WIDTH // MXU_TILE):
        cols = slice(MXU_TILE * m_, MXU_TILE * (m_ + 1))
        re = slice(half * m_, half * (m_ + 1))
        im = slice(N_STATE + half * m_, N_STATE + half * (m_ + 1))
        ys.append(_dot(hist[:, re].astype(BF16), cmat_ref[re, cols])
                  + _dot(hist[:, im].astype(BF16), cmat_ref[im, cols]))
    y = jnp.concatenate(ys, axis=1) + d_ref[...] * u
    y_ref[...] = _gelu_tanh(y).astype(BF16).reshape(y_ref.shape)
    ht_ref[...] = state[...]


def _s5(u_tm, h0, a_row, bmat, cmat, d_row, *, tc):
    nt, bs, _ = u_tm.shape
    rb = 16 if bs % 16 == 0 else 8
    assert bs % rb == 0 and nt % tc == 0
    return pl.pallas_call(
        functools.partial(_s5_kernel, tc=tc, bs=bs, rb=rb, lc=256),
        grid=(nt // tc,),
        in_specs=[pl.BlockSpec((tc, bs, S_WIDTH), lambda t: (t, 0, 0)),
                  _const_spec((bs, 2 * N_STATE)), _const_spec((1, 2 * N_STATE)),
                  _const_spec((S_WIDTH, 2 * N_STATE)), _const_spec((2 * N_STATE, S_WIDTH)),
                  _const_spec((1, S_WIDTH))],
        out_specs=[pl.BlockSpec((tc, bs, S_WIDTH), lambda t: (t, 0, 0)),
                   pl.BlockSpec((bs, 2 * N_STATE), lambda t: (0, 0))],
        out_shape=[jax.ShapeDtypeStruct((nt, bs, S_WIDTH), BF16),
                   jax.ShapeDtypeStruct((bs, 2 * N_STATE), F32)],
        scratch_shapes=[pltpu.VMEM((tc * bs, 2 * N_STATE), F32), pltpu.VMEM((bs, 2 * N_STATE), F32)],
        compiler_params=_cparams("arbitrary"),
        name="s5",
    )(u_tm, h0, a_row, bmat, cmat, d_row)


N_POST_IN = 17


def _post_kernel(*refs, bb, tt, ffc, sattn):
    if sattn is None:
        post_in, (y_ref, h2_sc, down_sc) = refs[:N_POST_IN], refs[N_POST_IN:]
    else:
        pt_ref, refs = refs[0], refs[1:]
        post_in, rest = refs[:N_POST_IN], refs[N_POST_IN:]
        sa_in, (y_ref, o_ref), (h2_sc, down_sc), sa_sc = rest[:7], rest[7:9], rest[9:11], rest[11:]
        sa_refs = (pt_ref,) + tuple(sa_in) + (o_ref,) + tuple(sa_sc)
    (x_ref, a_ref, ys_ref, sga_ref, sgs_ref, g1_ref, sh2_ref, sc2_ref, g2_ref, gffn_ref,
     wglu_ref, bglu_ref, wupa_ref, wups_ref, wout_ref, wffu_ref, wffd_ref) = post_in
    rows = bb * tt

    def mix():
        ys = ys_ref[...].reshape(rows, S_WIDTH)
        glu = _dot(ys, wglu_ref[...]) + bglu_ref[...]
        s = (ys.astype(F32) * jax.nn.sigmoid(glu)).astype(BF16)
        a = a_ref[...].reshape(rows, A_WIDTH)
        merged = (sga_ref[...].reshape(rows, D_MODEL).astype(F32) * _dot(a, wupa_ref[...])
                  + sgs_ref[...].reshape(rows, D_MODEL).astype(F32) * _dot(s, wups_ref[...]))
        x1 = x_ref[...] + g1_ref[...] * _dot(merged.astype(BF16), wout_ref[...]).reshape(bb, tt, D_MODEL)
        y_ref[...] = x1
        ms = jnp.mean(x1 * x1, axis=-1, keepdims=True)
        h2 = x1 * lax.rsqrt(ms + NORM_EPS) * gffn_ref[...]
        h2_sc[...] = (h2 * (1.0 + sc2_ref[...]) + sh2_ref[...]).reshape(rows, D_MODEL).astype(BF16)

    def ffn(c):
        up = jnp.maximum(_dot(h2_sc[...], wffu_ref[:, ffc * c:ffc * (c + 1)]), 0.0)
        part = _dot((up * up).astype(BF16), wffd_ref[ffc * c:ffc * (c + 1), :])
        if c == 0:
            down_sc[...] = part
        else:
            down_sc[...] += part

    work = [mix] + [functools.partial(ffn, c) for c in range(D_FF // ffc)]
    if sattn is None:
        for item in work:
            item()
    else:
        g = pl.program_id(0) * pl.num_programs(1) + pl.program_id(1)
        cps, done = sattn["cps"], 0
        for idx in range(cps):
            _sattn_chunk(g, idx, sa_refs, **sattn)
            upto = ((idx + 1) * len(work) + cps // 2) // cps
            for item in work[done:upto]:
                item()
            done = max(done, upto)
        assert done == len(work)
    y_ref[...] = y_ref[...] + g2_ref[...] * down_sc[...].reshape(bb, tt, D_MODEL)


def _post(x, a, ys, sga, sgs, g1, sh2, sc2, g2, gffn, wglu, bglu, wupa, wups, wout, wffu, wffd,
          *, bb, tt, ys_time_major, sattn_args=None, pp=8):
    nb, nt_total, _ = x.shape
    grid = (nb // bb, nt_total // tt)
    fused = sattn_args is not None

    def imap(f):
        return (lambda b, t, pt: f(b, t)) if fused else f

    def tok(width):
        return pl.BlockSpec((bb, tt, width), imap(lambda b, t: (b, t, 0)))

    def mod():
        return pl.BlockSpec((bb, 1, D_MODEL), imap(lambda b, t: (b, 0, 0)))

    def const(shape):
        nd = len(shape)
        return pl.BlockSpec(shape, imap(lambda b, t: (0,) * nd), pipeline_mode=pl.Buffered(1))

    if ys_time_major:
        assert bb == 1
        ys_spec = pl.BlockSpec((tt, S_WIDTH), imap(lambda b, t: (t, b)))
    else:
        ys_spec = tok(S_WIDTH)

    in_specs = [tok(D_MODEL), tok(A_WIDTH), ys_spec, tok(D_MODEL), tok(D_MODEL),
                mod(), mod(), mod(), mod(), const((1, D_MODEL)),
                const((S_WIDTH, S_WIDTH)), const((1, S_WIDTH)),
                const((A_WIDTH, D_MODEL)), const((S_WIDTH, D_MODEL)),
                const((D_MODEL, D_MODEL)), const((D_MODEL, D_FF)), const((D_FF, D_MODEL))]
    assert len(in_specs) == N_POST_IN
    operands = [x, a, ys, sga, sgs, g1, sh2, sc2, g2, gffn, wglu, bglu, wupa, wups, wout, wffu, wffd]
    out_specs = tok(D_MODEL)
    out_shape = jax.ShapeDtypeStruct((nb, nt_total, D_MODEL), F32)
    rows = bb * tt
    scratch = [pltpu.VMEM((rows, D_MODEL), BF16), pltpu.VMEM((rows, D_MODEL), F32)]
    static = None
    semantics = ("parallel", "parallel")
    if fused:
        page_table, qbd, qabd, kan, vn, cache_kt, cache_vt, rsuf = sattn_args
        nbs, n_pages = page_table.shape
        ts = kan.shape[1]
        nrow = N_HEADS * ts
        n_steps = grid[0] * grid[1]
        assert nbs % n_steps == 0 and n_pages % pp == 0
        spp = nbs // n_steps
        cps = spp * (n_pages // pp)
        assert cps >= PAGE_SLOTS
        static = dict(pp=pp, ts=ts, n_pages=n_pages, cps=cps)

        def per_step(shape):
            return pl.BlockSpec(shape, lambda b, t, pt: (b * grid[1] + t, 0, 0))

        hbm = pl.BlockSpec(memory_space=pl.ANY)
        in_specs += [per_step((spp, nrow, A_WIDTH)), per_step((spp, nrow, QA_WIDTH)),
                     per_step((spp, ts, QA_WIDTH)), per_step((spp, ts, A_WIDTH)), hbm, hbm, hbm]
        operands = [page_table.reshape(-1)] + operands + [qbd, qabd, kan, vn, cache_kt, cache_vt, rsuf]
        out_specs = [out_specs, per_step((spp, ts, A_WIDTH))]
        out_shape = [out_shape, jax.ShapeDtypeStruct((nbs, ts, A_WIDTH), BF16)]
        scratch += [pltpu.VMEM((PAGE_SLOTS, A_WIDTH, pp * PAGE), F32), pltpu.VMEM((PAGE_SLOTS, A_WIDTH, pp * PAGE), F32),
                    pltpu.VMEM((PAGE_SLOTS, pp, N_HEADS, 2 * PAGE), F32), pltpu.SemaphoreType.DMA((3, PAGE_SLOTS)),
                    pltpu.VMEM((nrow, 1), F32), pltpu.VMEM((nrow, 1), F32),
                    pltpu.VMEM((nrow, A_WIDTH), F32), pltpu.VMEM((N_HEADS, PAGE), F32)]
        semantics = ("arbitrary", "arbitrary")

    grid_spec = pltpu.PrefetchScalarGridSpec(
        num_scalar_prefetch=1 if fused else 0, grid=grid, in_specs=in_specs, out_specs=out_specs,
        scratch_shapes=scratch)
    return pl.pallas_call(
        functools.partial(_post_kernel, bb=bb, tt=tt, ffc=512, sattn=static),
        grid_spec=grid_spec,
        out_shape=out_shape,
        compiler_params=pltpu.CompilerParams(dimension_semantics=semantics, vmem_limit_bytes=POST_VMEM_LIMIT),
        name="post_sattn" if fused else "post",
    )(*operands)


def _block_diag_heads(q, width):
    nb, ts, _ = q.shape
    q5 = q.reshape(nb, ts, 1, N_HEADS, width)
    keep = jnp.eye(N_HEADS, dtype=bool).reshape(1, 1, N_HEADS, N_HEADS, 1)
    return jnp.where(keep, q5, jnp.zeros((), q.dtype)).reshape(nb, ts * N_HEADS, N_HEADS * width)


def kernel(x_prompt, x_sample, cache_k, cache_v, cache_logf, state_ssm_re, state_ssm_im, page_table,
           c_prompt, c_sample, w_ada, b_ada, norm_mix_g, norm_ffn_g, w_in, b_fgate, q_norm_g, k_norm_g,
           ssm_lambda_re, ssm_lambda_im, ssm_log_dt, ssm_b_re, ssm_b_im, ssm_c_re, ssm_c_im, ssm_d,
           w_glu, b_glu, w_up_a, w_up_s, w_out, w_ffn_up, w_ffn_down):
    nbp, ntp, _ = x_prompt.shape
    nbs, nts, _ = x_sample.shape
    n_pool = cache_k.shape[0]

    a3 = 3 * A_WIDTH
    wf_pad = jnp.pad(w_in[:, a3:a3 + N_HEADS], ((0, 0), (0, LANES - N_HEADS)))
    w2 = jnp.concatenate([w_in[:, :a3], w_in[:, a3 + N_HEADS:], wf_pad], axis=1).astype(BF16)
    bfp = jnp.pad(b_fgate, (0, LANES - N_HEADS)).reshape(1, LANES)
    wt = jnp.concatenate([w_in[:, :A_WIDTH].T, w_in[:, 2 * A_WIDTH:a3].T], axis=0).astype(BF16)
    gqp = jnp.pad(q_norm_g, (0, LANES - D_HEAD)).reshape(1, LANES)
    gq_col = q_norm_g.reshape(D_HEAD, 1)
    gkp = jnp.pad(k_norm_g, (0, LANES - D_HEAD)).reshape(1, LANES)
    heads = jnp.arange(N_HEADS)
    e2k = jnp.zeros((LANES, QA_WIDTH), F32)
    cq = jnp.zeros((1, QA_WIDTH), F32)
    for piece in range(N_SPLIT):
        e2k = e2k.at[piece * N_HEADS + heads, HEAD_PAD * heads + D_HEAD + piece].set(-1.0)
        cq = cq.at[0, HEAD_PAD * heads + D_HEAD + piece].set(1.0)
    e2k = e2k.astype(BF16)
    gmix = norm_mix_g.reshape(1, D_MODEL)
    gffn = norm_ffn_g.reshape(1, D_MODEL)
    post_w = (w_glu.astype(BF16), b_glu.reshape(1, S_WIDTH), w_up_a.astype(BF16), w_up_s.astype(BF16),
              w_out.astype(BF16), w_ffn_up.astype(BF16), w_ffn_down.astype(BF16))

    mod = _adaln(jnp.concatenate([c_prompt, c_sample], axis=0), w_ada.astype(BF16), b_ada)
    mod = mod.reshape(nbp + nbs, 1, 6, D_MODEL)
    mods_p = [mod[:nbp, :, i, :] for i in range(6)]
    mods_s = [mod[nbp:, :, i, :] for i in range(6)]

    ab_re, ab_im, bb_re, bb_im = _s5_disc(ssm_lambda_re, ssm_lambda_im, ssm_log_dt, ssm_b_re, ssm_b_im)
    a_row = jnp.concatenate([ab_re.reshape(1, N_STATE), ab_im.reshape(1, N_STATE)], axis=1)
    eye_g = jnp.eye(N_GROUPS, dtype=F32)

    def b_blockdiag(bb):
        return jnp.einsum('gcp,gh->gchp', bb, eye_g).reshape(S_WIDTH, N_STATE)

    def c_blockdiag(cc):
        return jnp.einsum('gcp,gh->gphc', cc, eye_g).reshape(N_STATE, S_WIDTH)

    bmat = jnp.concatenate([b_blockdiag(bb_re), b_blockdiag(bb_im)], axis=1).astype(BF16)
    cmat = jnp.concatenate([c_blockdiag(ssm_c_re), -c_blockdiag(ssm_c_im)], axis=0).astype(BF16)
    d_row = ssm_d.reshape(1, S_WIDTH)

    tt = min(512, ntp)
    ablk = min(512, ntp)
    bbs = min(64, nbs)
    qa_s, ka_s, k_s, v_s, vb_s, lf_s, u_s, sga_s, sgs_s, qn_s = _inproj(
        x_sample, mods_s[0], mods_s[1], gmix, w2, wt, bfp, gqp, gq_col, gkp, e2k, cq,
        bb=bbs, tt=nts, prompt=False, kblk=ablk)
    lf_rows = jnp.swapaxes(cache_logf, 1, 2).reshape(n_pool * N_HEADS, PAGE)
    rsuf = _pool_logf(lf_rows).reshape(n_pool, N_HEADS, 2 * PAGE)
    cache_kt = jnp.transpose(cache_k, (0, 2, 3, 1)).reshape(n_pool, A_WIDTH, PAGE)
    cache_vt = jnp.transpose(cache_v, (0, 2, 3, 1)).reshape(n_pool, A_WIDTH, PAGE)
    sattn_args = (page_table, _block_diag_heads(qn_s, D_HEAD), _block_diag_heads(qa_s, HEAD_PAD),
                  ka_s, vb_s, cache_kt, cache_vt, rsuf)

    qat, ka, k_p, vt_p, vtb, lf_p, u_tm, sga, sgs = _inproj(
        x_prompt, mods_p[0], mods_p[1], gmix, w2, wt, bfp, gqp, gq_col, gkp, e2k, cq,
        bb=1, tt=tt, prompt=True, kblk=ablk)
    attn_p = _attn(qat, ka, vtb, qblk=min(512, ntp), kblk=ablk)
    v_p = jnp.transpose(vt_p.reshape(nbp, N_HEADS, D_HEAD, ntp), (0, 3, 1, 2))
    h0 = jnp.zeros((nbp, 2 * N_STATE), F32)
    ys_tm, ht_p = _s5(u_tm.reshape(ntp, nbp, S_WIDTH), h0, a_row, bmat, cmat, d_row, tc=min(64, ntp))
    y_prompt, attn_s = _post(x_prompt, attn_p, ys_tm.reshape(ntp, nbp * S_WIDTH), sga, sgs,
                             mods_p[2], mods_p[3], mods_p[4], mods_p[5], gffn, *post_w,
                             bb=1, tt=tt, ys_time_major=True, sattn_args=sattn_args)

    h0_s = jnp.concatenate([state_ssm_re.reshape(nbs, N_STATE), state_ssm_im.reshape(nbs, N_STATE)], axis=1)
    ys_s_tm, ht_s = _s5(jnp.swapaxes(u_s, 0, 1), h0_s, a_row, bmat, cmat, d_row, tc=nts)
    y_sample = _post(x_sample, attn_s, jnp.swapaxes(ys_s_tm, 0, 1), sga_s, sgs_s,
                     mods_s[2], mods_s[3], mods_s[4], mods_s[5], gffn, *post_w,
                     bb=bbs, tt=nts, ys_time_major=False)

    def heads4(z):
        return z.reshape(z.shape[0], z.shape[1], N_HEADS, D_HEAD)

    def state3(z):
        return z.reshape(z.shape[0], N_GROUPS, STATE_P)

    return (y_prompt, y_sample, heads4(k_p), heads4(v_p), lf_p,
            state3(ht_p[:, :N_STATE]), state3(ht_p[:, N_STATE:]),
            heads4(k_s), heads4(v_s), lf_s,
            state3(ht_s[:, :N_STATE]), state3(ht_s[:, N_STATE:]))
```

```python
import functools
import math

import jax
import jax.numpy as jnp
from jax import lax
from jax.experimental import pallas as pl
from jax.experimental.pallas import tpu as pltpu

F32 = jnp.float32
BF16 = jnp.bfloat16

D_MODEL = 1024
N_HEADS = 8
D_HEAD = 64
A_WIDTH = N_HEADS * D_HEAD
S_WIDTH = D_MODEL // 2
GROUP_CH = 16
N_GROUPS = S_WIDTH // GROUP_CH
STATE_P = 64
N_STATE = N_GROUPS * STATE_P
D_FF = 4 * D_MODEL
PAGE = 128
ATTN_SCALE = 1.0 / math.sqrt(D_HEAD)
LOG2_E = math.log2(math.e)
NORM_EPS = 1e-6
NEG_BIG = -1e30

LANES = 128
MXU_TILE = 256
HEAD_PAD = 128
QA_WIDTH = N_HEADS * HEAD_PAD
N_SPLIT = 3
VMEM_LIMIT = 56 * 1024 * 1024
POST_VMEM_LIMIT = 60 * 1024 * 1024
PAGE_SLOTS = 2

_OQ, _OK, _OV, _OU, _OGA, _OGS, _OF, _W2 = 0, 512, 1024, 1536, 2048, 3072, 4096, 4224


def _cparams(*sem):
    return pltpu.CompilerParams(dimension_semantics=sem, vmem_limit_bytes=VMEM_LIMIT)


def _const_spec(shape):
    nd = len(shape)
    return pl.BlockSpec(shape, lambda *_: (0,) * nd, pipeline_mode=pl.Buffered(1))


def _split3(x):
    hi = x.astype(BF16)
    r1 = x - hi.astype(F32)
    mid = r1.astype(BF16)
    lo = (r1 - mid.astype(F32)).astype(BF16)
    return hi, mid, lo


def _dot(a, b):
    return jnp.dot(a, b, preferred_element_type=F32)


def _dot_nt(a, b):
    return lax.dot_general(a, b, (((1,), (1,)), ((), ())), preferred_element_type=F32)


def _adaln_kernel(c_ref, w_ref, b_ref, o_ref):
    c = c_ref[...]
    s = (c * jax.nn.sigmoid(c)).astype(BF16)
    o_ref[...] = _dot(s, w_ref[...]) + b_ref[...]


def _adaln(c_all, w_ada_bf, b_ada):
    n = c_all.shape[0]
    tn = 1024
    return pl.pallas_call(
        _adaln_kernel,
        grid=(6 * D_MODEL // tn,),
        in_specs=[pl.BlockSpec((n, D_MODEL), lambda j: (0, 0)),
                  pl.BlockSpec((D_MODEL, tn), lambda j: (0, j)),
                  pl.BlockSpec((1, tn), lambda j: (0, j))],
        out_specs=pl.BlockSpec((n, tn), lambda j: (0, j)),
        out_shape=jax.ShapeDtypeStruct((n, 6 * D_MODEL), F32),
        compiler_params=_cparams("parallel"),
        name="adaln",
    )(c_all, w_ada_bf, b_ada.reshape(1, -1))


def _s5_disc_kernel(lr_ref, li_ref, ldt_ref, br_ref, bi_ref, abr_ref, abi_ref, bbr_ref, bbi_ref):
    lr = lr_ref[...]
    li = li_ref[...]
    dt = jnp.exp(ldt_ref[...])
    mag = jnp.exp(lr * dt)
    ab_re = mag * jnp.cos(li * dt)
    ab_im = mag * jnp.sin(li * dt)
    nr, ni = ab_re - 1.0, ab_im
    den = lr * lr + li * li
    f_re = (nr * lr + ni * li) / den
    f_im = (ni * lr - nr * li) / den
    br = br_ref[...]
    bi = bi_ref[...]
    abr_ref[...] = ab_re
    abi_ref[...] = ab_im
    bbr_ref[...] = f_re * br - f_im * bi
    bbi_ref[...] = f_re * bi + f_im * br


def _s5_disc(lam_re, lam_im, log_dt, b_re, b_im):
    g3 = jax.ShapeDtypeStruct((N_GROUPS, 1, STATE_P), F32)
    b3 = jax.ShapeDtypeStruct((N_GROUPS, GROUP_CH, STATE_P), F32)
    return pl.pallas_call(
        _s5_disc_kernel,
        out_shape=(g3, g3, b3, b3),
        name="s5_disc",
    )(lam_re.reshape(N_GROUPS, 1, STATE_P), lam_im.reshape(N_GROUPS, 1, STATE_P),
      log_dt.reshape(N_GROUPS, 1, 1),
      jnp.swapaxes(b_re, 1, 2), jnp.swapaxes(b_im, 1, 2))


def _inproj_kernel(x_ref, sh_ref, sc_ref, gmix_ref, w_ref, wt_ref, bf_ref, gq_ref, gqcol_ref, gk_ref,
                   e2k_ref, cq_ref, tri_ref, *rest, bb, tt, prompt, kblk):
    if prompt:
        qat_ref, ka_ref, k_ref, vt_ref, vtb_ref, lf_ref, u_ref, sga_ref, sgs_ref, carry_ref = rest
    else:
        qa_ref, ka_ref, k_ref, v_ref, vb_ref, lf_ref, u_ref, sga_ref, sgs_ref, qn_ref, carry_ref = rest
    rows = bb * tt
    t = pl.program_id(1)

    x = x_ref[...]
    ms = jnp.mean(x * x, axis=-1, keepdims=True)
    h = x * lax.rsqrt(ms + NORM_EPS) * gmix_ref[...]
    h = h * (1.0 + sc_ref[...]) + sh_ref[...]
    hb = h.reshape(rows, D_MODEL).astype(BF16)

    lane = lax.broadcasted_iota(jnp.int32, (1, LANES), 1)
    low_half = lane < D_HEAD

    zf = _dot(hb, w_ref[:, _OF:_W2]) + bf_ref[...]
    lf = jnp.minimum(zf, 0.0) - jnp.log1p(jnp.exp(-jnp.abs(zf)))
    lf = jnp.where(lane < N_HEADS, lf, 0.0)
    lf_ref[...] = lf[:, :N_HEADS].reshape(lf_ref.shape)
    hi, mid, lo = _split3(lf)
    packed = (hi.astype(F32) + pltpu.roll(mid.astype(F32), N_HEADS, 1)
              + pltpu.roll(lo.astype(F32), 2 * N_HEADS, 1)).astype(BF16)
    gc = _dot(tri_ref[...], packed)
    g = gc + pltpu.roll(gc, LANES - N_HEADS, 1) + pltpu.roll(gc, LANES - 2 * N_HEADS, 1)
    g = jnp.where(lane < N_HEADS, g, 0.0)
    if bb == 1:
        @pl.when(t == 0)
        def _():
            carry_ref[...] = jnp.zeros_like(carry_ref)
        g = g + carry_ref[...]
        carry_ref[...] = g[rows - 1:rows, :]
    logit_scale = LOG2_E if prompt else 1.0
    ghi, gmid, glo = _split3(g * logit_scale)
    gpacked = (ghi.astype(F32) + pltpu.roll(gmid.astype(F32), N_HEADS, 1)
               + pltpu.roll(glo.astype(F32), 2 * N_HEADS, 1)).astype(BF16)
    kaug = _dot(gpacked, e2k_ref[...])

    def widen_norm(z, gain):
        out = []
        for part in (z, pltpu.roll(z, D_HEAD, 1)):
            e = jnp.where(low_half, part, 0.0)
            ss = jnp.sum(e * e, axis=-1, keepdims=True)
            out.append(e * lax.rsqrt(ss * (1.0 / D_HEAD) + NORM_EPS) * gain)
        return out

    gq = gq_ref[...] * ATTN_SCALE
    gk = gk_ref[...]
    for j in range(N_HEADS // 2):
        zk = _dot(hb, w_ref[:, _OK + LANES * j:_OK + LANES * (j + 1)])
        k_even, k_odd = widen_norm(zk, gk)
        for i, kh in enumerate((k_even, k_odd)):
            c0 = HEAD_PAD * (2 * j + i)
            ka_ref[..., c0:c0 + HEAD_PAD] = (kh + kaug[:, c0:c0 + HEAD_PAD]).astype(BF16).reshape(bb, tt, HEAD_PAD)
        k_ref[..., LANES * j:LANES * (j + 1)] = (k_even + pltpu.roll(k_odd, D_HEAD, 1)).reshape(bb, tt, LANES)
        if not prompt:
            zq = _dot(hb, w_ref[:, _OQ + LANES * j:_OQ + LANES * (j + 1)])
            q_even, q_odd = widen_norm(zq, gq)
            for i, qh in enumerate((q_even, q_odd)):
                c0 = HEAD_PAD * (2 * j + i)
                qa_ref[..., c0:c0 + HEAD_PAD] = (
                    (qh + cq_ref[:, c0:c0 + HEAD_PAD]).astype(BF16).reshape(bb, tt, HEAD_PAD))
            qn_ref[..., LANES * j:LANES * (j + 1)] = (
                (q_even + pltpu.roll(q_odd, D_HEAD, 1)).astype(BF16).reshape(bb, tt, LANES))

    if prompt:
        zqt = _dot_nt(wt_ref[:A_WIDTH, :], hb)
        gq_col = gqcol_ref[...] * (ATTN_SCALE * logit_scale)
        sub = lax.broadcasted_iota(jnp.int32, (D_HEAD, tt), 0)
        ones_rows = jnp.where(sub < N_SPLIT, 1.0, 0.0).astype(BF16)
        for hd in range(N_HEADS):
            z = zqt[D_HEAD * hd:D_HEAD * (hd + 1), :]
            ss = jnp.sum(z * z, axis=0, keepdims=True)
            qn_t = z * lax.rsqrt(ss * (1.0 / D_HEAD) + NORM_EPS) * gq_col
            qat_ref[0, HEAD_PAD * hd:HEAD_PAD * hd + D_HEAD, :] = qn_t.astype(BF16)
            qat_ref[0, HEAD_PAD * hd + D_HEAD:HEAD_PAD * (hd + 1), :] = ones_rows
        zvt = _dot_nt(wt_ref[A_WIDTH:, :], hb)
        vt_ref[0] = zvt
        for c in range(tt // kblk):
            vtb_ref[0, c] = zvt[:, kblk * c:kblk * (c + 1)].astype(BF16)
    else:
        zv = _dot(hb, w_ref[:, _OV:_OU])
        v_ref[...] = zv.reshape(v_ref.shape)
        vb_ref[...] = zv.astype(BF16).reshape(vb_ref.shape)
    u_ref[...] = _dot(hb, w_ref[:, _OU:_OGA]).reshape(u_ref.shape)
    sga_ref[...] = jax.nn.sigmoid(_dot(hb, w_ref[:, _OGA:_OGS])).astype(BF16).reshape(sga_ref.shape)
    sgs_ref[...] = jax.nn.sigmoid(_dot(hb, w_ref[:, _OGS:_OF])).astype(BF16).reshape(sgs_ref.shape)


def _inproj(x, sh1, sc1, gmix, w2, wt, bfp, gqp, gq_col, gkp, e2k, cq, *, bb, tt, prompt, kblk):
    nb, nt_total, _ = x.shape
    u_time_major = prompt
    rows = bb * tt
    idx = jnp.arange(rows)
    tri = ((idx[:, None] >= idx[None, :]) & (idx[:, None] // tt == idx[None, :] // tt)).astype(BF16)
    grid = (nb // bb, nt_total // tt)
    assert bb == 1 or grid[1] == 1

    def tok(width):
        return pl.BlockSpec((bb, tt, width), lambda b, t: (b, t, 0))

    def mod():
        return pl.BlockSpec((bb, 1, D_MODEL), lambda b, t: (b, 0, 0))

    if u_time_major:
        assert bb == 1
        u_shape = jax.ShapeDtypeStruct((nt_total, nb * S_WIDTH), F32)
        u_spec = pl.BlockSpec((tt, S_WIDTH), lambda b, t: (t, b))
    else:
        u_shape = jax.ShapeDtypeStruct((nb, nt_total, S_WIDTH), F32)
        u_spec = tok(S_WIDTH)

    def sds(width, dt):
        return jax.ShapeDtypeStruct((nb, nt_total, width), dt)

    tail_shape = [sds(N_HEADS, F32), u_shape, sds(D_MODEL, BF16), sds(D_MODEL, BF16)]
    tail_specs = [tok(N_HEADS), u_spec, tok(D_MODEL), tok(D_MODEL)]
    if prompt:
        assert tt % kblk == 0
        out_shape = [jax.ShapeDtypeStruct((nb, QA_WIDTH, nt_total), BF16), sds(QA_WIDTH, BF16), sds(A_WIDTH, F32),
                     jax.ShapeDtypeStruct((nb, A_WIDTH, nt_total), F32),
                     jax.ShapeDtypeStruct((nb, nt_total // kblk, A_WIDTH, kblk), BF16)] + tail_shape
        out_specs = [pl.BlockSpec((1, QA_WIDTH, tt), lambda b, t: (b, 0, t)), tok(QA_WIDTH), tok(A_WIDTH),
                     pl.BlockSpec((1, A_WIDTH, tt), lambda b, t: (b, 0, t)),
                     pl.BlockSpec((1, tt // kblk, A_WIDTH, kblk), lambda b, t: (b, t, 0, 0))] + tail_specs
    else:
        out_shape = ([sds(QA_WIDTH, BF16), sds(QA_WIDTH, BF16), sds(A_WIDTH, F32), sds(A_WIDTH, F32),
                      sds(A_WIDTH, BF16)] + tail_shape + [sds(A_WIDTH, BF16)])
        out_specs = ([tok(QA_WIDTH), tok(QA_WIDTH), tok(A_WIDTH), tok(A_WIDTH), tok(A_WIDTH)]
                     + tail_specs + [tok(A_WIDTH)])

    return pl.pallas_call(
        functools.partial(_inproj_kernel, bb=bb, tt=tt, prompt=prompt, kblk=kblk),
        grid=grid,
        in_specs=[tok(D_MODEL), mod(), mod(), _const_spec((1, D_MODEL)), _const_spec((D_MODEL, _W2)),
                  _const_spec((2 * A_WIDTH, D_MODEL)),
                  _const_spec((1, LANES)), _const_spec((1, LANES)), _const_spec((D_HEAD, 1)), _const_spec((1, LANES)),
                  _const_spec((LANES, QA_WIDTH)), _const_spec((1, QA_WIDTH)), _const_spec((rows, rows))],
        out_specs=out_specs,
        out_shape=out_shape,
        scratch_shapes=[pltpu.VMEM((1, LANES), F32)],
        compiler_params=_cparams("parallel", "arbitrary"),
        name="inproj",
    )(x, sh1, sc1, gmix, w2, wt, bfp, gqp, gq_col, gkp, e2k, cq, tri)


def _attn_kernel(qat_ref, ka_ref, vtb_ref, o_ref, m_sc, l_sc, acc_sc, *, qblk, kblk):
    qi = pl.program_id(1)
    ratio = qblk // kblk
    m_sc[...] = jnp.full_like(m_sc, NEG_BIG)
    l_sc[...] = jnp.zeros_like(l_sc)
    acc_sc[...] = jnp.zeros_like(acc_sc)
    key = lax.broadcasted_iota(jnp.int32, (kblk, qblk), 0)
    qry = lax.broadcasted_iota(jnp.int32, (kblk, qblk), 1)

    def block(kb, diag):
        ks = pl.multiple_of(kb * kblk, kblk)
        for h in range(N_HEADS):
            k = ka_ref[0, pl.ds(ks, kblk), HEAD_PAD * h:HEAD_PAD * (h + 1)]
            st = _dot(k, qat_ref[0, HEAD_PAD * h:HEAD_PAD * (h + 1), :])
            if diag is not None:
                st = jnp.where(key + diag * kblk <= qry, st, NEG_BIG)
            m_old = m_sc[h]
            m_new = jnp.maximum(m_old, jnp.max(st, axis=0, keepdims=True))
            alpha = jnp.exp2(m_old - m_new)
            p = jnp.exp2(st - m_new)
            l_sc[h] = alpha * l_sc[h] + jnp.sum(p, axis=0, keepdims=True)
            vt = vtb_ref[0, kb, D_HEAD * h:D_HEAD * (h + 1), :]
            rows = slice(D_HEAD * h, D_HEAD * (h + 1))
            acc_sc[rows, :] = alpha * acc_sc[rows, :] + _dot(vt, p.astype(BF16))
            m_sc[h] = m_new

    def body(kb, carry):
        block(kb, None)
        return carry

    lax.fori_loop(0, qi * ratio, body, 0)
    for d in range(ratio):
        block(qi * ratio + d, d)

    for hp in range(N_HEADS // 2):
        inv = jnp.concatenate([jnp.broadcast_to(1.0 / l_sc[2 * hp + i], (D_HEAD, qblk)) for i in range(2)], axis=0)
        pair = acc_sc[LANES * hp:LANES * (hp + 1), :] * inv
        o_ref[0, :, LANES * hp:LANES * (hp + 1)] = pair.T.astype(BF16)


def _attn(qat, ka, vtb, *, qblk, kblk):
    nb, nt, _ = ka.shape
    assert vtb.shape == (nb, nt // kblk, A_WIDTH, kblk) and qblk % kblk == 0
    return pl.pallas_call(
        functools.partial(_attn_kernel, qblk=qblk, kblk=kblk),
        grid=(nb, nt // qblk),
        in_specs=[pl.BlockSpec((1, QA_WIDTH, qblk), lambda b, i: (b, 0, i)),
                  pl.BlockSpec((1, nt, QA_WIDTH), lambda b, i: (b, 0, 0)),
                  pl.BlockSpec((1, nt // kblk, A_WIDTH, kblk), lambda b, i: (b, 0, 0, 0))],
        out_specs=pl.BlockSpec((1, qblk, A_WIDTH), lambda b, i: (b, i, 0)),
        out_shape=jax.ShapeDtypeStruct((nb, nt, A_WIDTH), BF16),
        scratch_shapes=[pltpu.VMEM((N_HEADS, 1, qblk), F32), pltpu.VMEM((N_HEADS, 1, qblk), F32),
                        pltpu.VMEM((A_WIDTH, qblk), F32)],
        compiler_params=_cparams("parallel", "arbitrary"),
        name="attn",
    )(qat, ka, vtb)


def _pool_logf_kernel(lf_ref, m_ref, o_ref):
    hi, mid, lo = _split3(lf_ref[...])
    m = m_ref[...]
    o_ref[...] = _dot(hi, m) + _dot(mid, m) + _dot(lo, m)


def _pool_logf(lf_rows):
    n_rows = lf_rows.shape[0]
    tile = next(c for c in (2048, 1024, 512, 256, 128, 64, 32, 16, 8) if n_rows % c == 0)
    j_src = jnp.arange(PAGE)
    suffix = j_src[:, None] > j_src[None, :]
    m = jnp.concatenate([suffix, jnp.ones((PAGE, PAGE), bool)], axis=1).astype(BF16)
    return pl.pallas_call(
        _pool_logf_kernel,
        grid=(n_rows // tile,),
        in_specs=[pl.BlockSpec((tile, PAGE), lambda i: (i, 0)), _const_spec((PAGE, 2 * PAGE))],
        out_specs=pl.BlockSpec((tile, 2 * PAGE), lambda i: (i, 0)),
        out_shape=jax.ShapeDtypeStruct((n_rows, 2 * PAGE), F32),
        compiler_params=_cparams("parallel"),
        name="pool_logf",
    )(lf_rows, m)


def _sattn_chunk(g, idx, refs, *, pp, ts, n_pages, cps):
    (pt_ref, qbd_ref, qabd_ref, kan_ref, vn_ref, k_hbm, v_hbm, r_hbm, o_ref,
     kpage, vpage, rpage, sem, m_sc, l_sc, acc_sc, tot_sc) = refs
    cpq = n_pages // pp
    spp = cps // cpq
    n_steps = pl.num_programs(0) * pl.num_programs(1)
    nrow = N_HEADS * ts

    def page_copies(ahead):
        g_, idx_ = g + (idx + ahead) // cps, (idx + ahead) % cps
        seq, chunk = g_ * spp + idx_ // cpq, idx_ % cpq
        slot_ = lax.rem(g_ * cps + idx_, PAGE_SLOTS)
        out = []
        for i in range(pp):
            pid = pt_ref[seq * n_pages + (n_pages - 1 - (chunk * pp + i))]
            lanes = pl.ds(PAGE * i, PAGE)
            out.append(pltpu.make_async_copy(k_hbm.at[pid], kpage.at[slot_, :, lanes], sem.at[0, slot_]))
            out.append(pltpu.make_async_copy(v_hbm.at[pid], vpage.at[slot_, :, lanes], sem.at[1, slot_]))
            out.append(pltpu.make_async_copy(r_hbm.at[pid], rpage.at[slot_, i], sem.at[2, slot_]))
        return out

    depth = PAGE_SLOTS - 1
    if idx == 0:
        @pl.when(g == 0)
        def _():
            for ahead in range(depth):
                for cp in page_copies(ahead):
                    cp.start()
    if idx + depth < cps:
        for cp in page_copies(depth):
            cp.start()
    else:
        @pl.when(g + 1 < n_steps)
        def _():
            for cp in page_copies(depth):
                cp.start()
    for cp in page_copies(0):
        cp.wait()

    slot = lax.rem(g * cps + idx, PAGE_SLOTS)
    seq_l, chunk = idx // cpq, idx % cpq
    if chunk == 0:
        m_sc[...] = jnp.full_like(m_sc, NEG_BIG)
        l_sc[...] = jnp.zeros_like(l_sc)
        acc_sc[...] = jnp.zeros_like(acc_sc)
        tot_sc[...] = jnp.zeros_like(tot_sc)

    def update(s, pv):
        m_old = m_sc[...]
        m_new = jnp.maximum(m_old, jnp.max(s, axis=-1, keepdims=True))
        alpha = jnp.exp(m_old - m_new)
        p = jnp.exp(s - m_new)
        l_sc[...] = alpha * l_sc[...] + jnp.sum(p, axis=-1, keepdims=True)
        acc_sc[...] = alpha * acc_sc[...] + pv(p)
        m_sc[...] = m_new

    later = tot_sc[...]
    biases = []
    for i in range(pp):
        r = rpage[slot, i]
        biases.append(jnp.concatenate([r[:, :PAGE] + later] * ts, axis=0))
        later = later + r[:, PAGE:]
    tot_sc[...] = later
    s = _dot(qbd_ref[seq_l].astype(F32), kpage[slot]) + jnp.concatenate(biases, axis=1)
    update(s, lambda p: _dot_nt(p, vpage[slot]))

    if chunk == cpq - 1:
        s_new = _dot_nt(qabd_ref[seq_l], kan_ref[seq_l])
        r_i = lax.broadcasted_iota(jnp.int32, (nrow, ts), 0)
        c_i = lax.broadcasted_iota(jnp.int32, (nrow, ts), 1)
        update(jnp.where((r_i // N_HEADS) >= c_i, s_new, NEG_BIG), lambda p: _dot(p.astype(BF16), vn_ref[seq_l]))
        o = acc_sc[...] / l_sc[...]
        lane_head = lax.broadcasted_iota(jnp.int32, (N_HEADS, A_WIDTH), 1) // D_HEAD
        own = lane_head == lax.broadcasted_iota(jnp.int32, (N_HEADS, A_WIDTH), 0)
        rows = [jnp.sum(jnp.where(own, o[N_HEADS * t:N_HEADS * (t + 1), :], 0.0), axis=0, keepdims=True)
                for t in range(ts)]
        o_ref[seq_l] = jnp.concatenate(rows, axis=0).astype(BF16)


def _gelu_tanh(x):
    return 0.5 * x * (1.0 + jnp.tanh(math.sqrt(2.0 / math.pi) * (x + 0.044715 * (x * x * x))))


def _s5_kernel(u_ref, h0_ref, a_ref, bmat_ref, cmat_ref, d_ref, y_ref, ht_ref, hist, state,
               *, tc, bs, rb, lc):
    t = pl.program_id(0)

    @pl.when(t == 0)
    def _():
        state[...] = h0_ref[...]

    u = u_ref[...].reshape(tc * bs, S_WIDTH)
    ub = u.astype(BF16)
    for n in range(2 * N_STATE // MXU_TILE):
        slab = LANES * ((n % (N_STATE // MXU_TILE)) // 2)
        cols = slice(MXU_TILE * n, MXU_TILE * (n + 1))
        hist[:, cols] = _dot(ub[:, slab:slab + LANES], bmat_ref[slab:slab + LANES, cols])

    for c in range(N_STATE // lc):
        re = slice(lc * c, lc * (c + 1))
        im = slice(N_STATE + lc * c, N_STATE + lc * (c + 1))
        a_re = jnp.broadcast_to(a_ref[:, re], (rb, lc))
        a_im = jnp.broadcast_to(a_ref[:, im], (rb, lc))

        def row_block(r, _, re=re, im=im, a_re=a_re, a_im=a_im):
            r0 = pl.multiple_of(r * rb, rb)

            def step(j, carry):
                h_re, h_im = carry
                row = pl.multiple_of(j * bs + r0, rb)
                n_re = a_re * h_re - a_im * h_im + hist[pl.ds(row, rb), re]
                n_im = a_re * h_im + a_im * h_re + hist[pl.ds(row, rb), im]
                hist[pl.ds(row, rb), re] = n_re
                hist[pl.ds(row, rb), im] = n_im
                return n_re, n_im

            h_re, h_im = lax.fori_loop(0, tc, step, (state[pl.ds(r0, rb), re], state[pl.ds(r0, rb), im]))
            state[pl.ds(r0, rb), re] = h_re
            state[pl.ds(r0, rb), im] = h_im
            return 0

        lax.fori_loop(0, bs // rb, row_block, 0)

    half = N_STATE // (S_WIDTH // MXU_TILE)
    ys = []
    for m_ in range(S_WIDTH // MXU_TILE):
        cols = slice(MXU_TILE * m_, MXU_TILE * (m_ + 1))
        re = slice(half * m_, half * (m_ + 1))
        im = slice(N_STATE + half * m_, N_STATE + half * (m_ + 1))
        ys.append(_dot(hist[:, re].astype(BF16), cmat_ref[re, cols])
                  + _dot(hist[:, im].astype(BF16), cmat_ref[im, cols]))
    y = jnp.concatenate(ys, axis=1) + d_ref[...] * u
    y_ref[...] = _gelu_tanh(y).astype(BF16).reshape(y_ref.shape)
    ht_ref[...] = state[...]


def _s5(u_tm, h0, a_row, bmat, cmat, d_row, *, tc):
    nt, bs, _ = u_tm.shape
    rb = 16 if bs % 16 == 0 else 8
    assert bs % rb == 0 and nt % tc == 0
    return pl.pallas_call(
        functools.partial(_s5_kernel, tc=tc, bs=bs, rb=rb, lc=512),
        grid=(nt // tc,),
        in_specs=[pl.BlockSpec((tc, bs, S_WIDTH), lambda t: (t, 0, 0)),
                  _const_spec((bs, 2 * N_STATE)), _const_spec((1, 2 * N_STATE)),
                  _const_spec((S_WIDTH, 2 * N_STATE)), _const_spec((2 * N_STATE, S_WIDTH)),
                  _const_spec((1, S_WIDTH))],
        out_specs=[pl.BlockSpec((tc, bs, S_WIDTH), lambda t: (t, 0, 0)),
                   pl.BlockSpec((bs, 2 * N_STATE), lambda t: (0, 0))],
        out_shape=[jax.ShapeDtypeStruct((nt, bs, S_WIDTH), BF16),
                   jax.ShapeDtypeStruct((bs, 2 * N_STATE), F32)],
        scratch_shapes=[pltpu.VMEM((tc * bs, 2 * N_STATE), F32), pltpu.VMEM((bs, 2 * N_STATE), F32)],
        compiler_params=_cparams("arbitrary"),
        name="s5",
    )(u_tm, h0, a_row, bmat, cmat, d_row)


N_POST_IN = 17


def _post_kernel(*refs, bb, tt, ffc, sattn):
    if sattn is None:
        post_in, (y_ref, h2_sc, down_sc) = refs[:N_POST_IN], refs[N_POST_IN:]
    else:
        pt_ref, refs = refs[0], refs[1:]
        post_in, rest = refs[:N_POST_IN], refs[N_POST_IN:]
        sa_in, (y_ref, o_ref), (h2_sc, down_sc), sa_sc = rest[:7], rest[7:9], rest[9:11], rest[11:]
        sa_refs = (pt_ref,) + tuple(sa_in) + (o_ref,) + tuple(sa_sc)
    (x_ref, a_ref, ys_ref, sga_ref, sgs_ref, g1_ref, sh2_ref, sc2_ref, g2_ref, gffn_ref,
     wglu_ref, bglu_ref, wupa_ref, wups_ref, wout_ref, wffu_ref, wffd_ref) = post_in
    rows = bb * tt

    def mix():
        ys = ys_ref[...].reshape(rows, S_WIDTH)
        glu = _dot(ys, wglu_ref[...]) + bglu_ref[...]
        s = (ys.astype(F32) * jax.nn.sigmoid(glu)).astype(BF16)
        a = a_ref[...].reshape(rows, A_WIDTH)
        merged = (sga_ref[...].reshape(rows, D_MODEL).astype(F32) * _dot(a, wupa_ref[...])
                  + sgs_ref[...].reshape(rows, D_MODEL).astype(F32) * _dot(s, wups_ref[...]))
        x1 = x_ref[...] + g1_ref[...] * _dot(merged.astype(BF16), wout_ref[...]).reshape(bb, tt, D_MODEL)
        y_ref[...] = x1
        ms = jnp.mean(x1 * x1, axis=-1, keepdims=True)
        h2 = x1 * lax.rsqrt(ms + NORM_EPS) * gffn_ref[...]
        h2_sc[...] = (h2 * (1.0 + sc2_ref[...]) + sh2_ref[...]).reshape(rows, D_MODEL).astype(BF16)

    def ffn(c):
        up = jnp.maximum(_dot(h2_sc[...], wffu_ref[:, ffc * c:ffc * (c + 1)]), 0.0)
        part = _dot((up * up).astype(BF16), wffd_ref[ffc * c:ffc * (c + 1), :])
        if c == 0:
            down_sc[...] = part
        else:
            down_sc[...] += part

    work = [mix] + [functools.partial(ffn, c) for c in range(D_FF // ffc)]
    if sattn is None:
        for item in work:
            item()
    else:
        g = pl.program_id(0) * pl.num_programs(1) + pl.program_id(1)
        cps, done = sattn["cps"], 0
        for idx in range(cps):
            _sattn_chunk(g, idx, sa_refs, **sattn)
            upto = ((idx + 1) * len(work) + cps // 2) // cps
            for item in work[done:upto]:
                item()
            done = max(done, upto)
        assert done == len(work)
    y_ref[...] = y_ref[...] + g2_ref[...] * down_sc[...].reshape(bb, tt, D_MODEL)


def _post(x, a, ys, sga, sgs, g1, sh2, sc2, g2, gffn, wglu, bglu, wupa, wups, wout, wffu, wffd,
          *, bb, tt, ys_time_major, sattn_args=None, pp=16):
    nb, nt_total, _ = x.shape
    grid = (nb // bb, nt_total // tt)
    fused = sattn_args is not None

    def imap(f):
        return (lambda b, t, pt: f(b, t)) if fused else f

    def tok(width):
        return pl.BlockSpec((bb, tt, width), imap(lambda b, t: (b, t, 0)))

    def mod():
        return pl.BlockSpec((bb, 1, D_MODEL), imap(lambda b, t: (b, 0, 0)))

    def const(shape):
        nd = len(shape)
        return pl.BlockSpec(shape, imap(lambda b, t: (0,) * nd), pipeline_mode=pl.Buffered(1))

    if ys_time_major:
        assert bb == 1
        ys_spec = pl.BlockSpec((tt, S_WIDTH), imap(lambda b, t: (t, b)))
    else:
        ys_spec = tok(S_WIDTH)

    in_specs = [tok(D_MODEL), tok(A_WIDTH), ys_spec, tok(D_MODEL), tok(D_MODEL),
                mod(), mod(), mod(), mod(), const((1, D_MODEL)),
                const((S_WIDTH, S_WIDTH)), const((1, S_WIDTH)),
                const((A_WIDTH, D_MODEL)), const((S_WIDTH, D_MODEL)),
                const((D_MODEL, D_MODEL)), const((D_MODEL, D_FF)), const((D_FF, D_MODEL))]
    assert len(in_specs) == N_POST_IN
    operands = [x, a, ys, sga, sgs, g1, sh2, sc2, g2, gffn, wglu, bglu, wupa, wups, wout, wffu, wffd]
    out_specs = tok(D_MODEL)
    out_shape = jax.ShapeDtypeStruct((nb, nt_total, D_MODEL), F32)
    rows = bb * tt
    scratch = [pltpu.VMEM((rows, D_MODEL), BF16), pltpu.VMEM((rows, D_MODEL), F32)]
    static = None
    semantics = ("parallel", "parallel")
    if fused:
        page_table, qbd, qabd, kan, vn, cache_kt, cache_vt, rsuf = sattn_args
        nbs, n_pages = page_table.shape
        ts = kan.shape[1]
        nrow = N_HEADS * ts
        n_steps = grid[0] * grid[1]
        assert nbs % n_steps == 0 and n_pages % pp == 0
        spp = nbs // n_steps
        cps = spp * (n_pages // pp)
        assert cps >= PAGE_SLOTS
        static = dict(pp=pp, ts=ts, n_pages=n_pages, cps=cps)

        def per_step(shape):
            return pl.BlockSpec(shape, lambda b, t, pt: (b * grid[1] + t, 0, 0))

        hbm = pl.BlockSpec(memory_space=pl.ANY)
        in_specs += [per_step((spp, nrow, A_WIDTH)), per_step((spp, nrow, QA_WIDTH)),
                     per_step((spp, ts, QA_WIDTH)), per_step((spp, ts, A_WIDTH)), hbm, hbm, hbm]
        operands = [page_table.reshape(-1)] + operands + [qbd, qabd, kan, vn, cache_kt, cache_vt, rsuf]
        out_specs = [out_specs, per_step((spp, ts, A_WIDTH))]
        out_shape = [out_shape, jax.ShapeDtypeStruct((nbs, ts, A_WIDTH), BF16)]
        scratch += [pltpu.VMEM((PAGE_SLOTS, A_WIDTH, pp * PAGE), F32), pltpu.VMEM((PAGE_SLOTS, A_WIDTH, pp * PAGE), F32),
                    pltpu.VMEM((PAGE_SLOTS, pp, N_HEADS, 2 * PAGE), F32), pltpu.SemaphoreType.DMA((3, PAGE_SLOTS)),
                    pltpu.VMEM((nrow, 1), F32), pltpu.VMEM((nrow, 1), F32),
                    pltpu.VMEM((nrow, A_WIDTH), F32), pltpu.VMEM((N_HEADS, PAGE), F32)]
        semantics = ("arbitrary", "arbitrary")

    grid_spec = pltpu.PrefetchScalarGridSpec(
        num_scalar_prefetch=1 if fused else 0, grid=grid, in_specs=in_specs, out_specs=out_specs,
        scratch_shapes=scratch)
    return pl.pallas_call(
        functools.partial(_post_kernel, bb=bb, tt=tt, ffc=512, sattn=static),
        grid_spec=grid_spec,
        out_shape=out_shape,
        compiler_params=pltpu.CompilerParams(dimension_semantics=semantics, vmem_limit_bytes=POST_VMEM_LIMIT),
        name="post_sattn" if fused else "post",
    )(*operands)


def _block_diag_heads(q, width):
    nb, ts, _ = q.shape
    q5 = q.reshape(nb, ts, 1, N_HEADS, width)
    keep = jnp.eye(N_HEADS, dtype=bool).reshape(1, 1, N_HEADS, N_HEADS, 1)
    return jnp.where(keep, q5, jnp.zeros((), q.dtype)).reshape(nb, ts * N_HEADS, N_HEADS * width)


def kernel(x_prompt, x_sample, cache_k, cache_v, cache_logf, state_ssm_re, state_ssm_im, page_table,
           c_prompt, c_sample, w_ada, b_ada, norm_mix_g, norm_ffn_g, w_in, b_fgate, q_norm_g, k_norm_g,
           ssm_lambda_re, ssm_lambda_im, ssm_log_dt, ssm_b_re, ssm_b_im, ssm_c_re, ssm_c_im, ssm_d,
           w_glu, b_glu, w_up_a, w_up_s, w_out, w_ffn_up, w_ffn_down):
    nbp, ntp, _ = x_prompt.shape
    nbs, nts, _ = x_sample.shape
    n_pool = cache_k.shape[0]

    a3 = 3 * A_WIDTH
    wf_pad = jnp.pad(w_in[:, a3:a3 + N_HEADS], ((0, 0), (0, LANES - N_HEADS)))
    w2 = jnp.concatenate([w_in[:, :a3], w_in[:, a3 + N_HEADS:], wf_pad], axis=1).astype(BF16)
    bfp = jnp.pad(b_fgate, (0, LANES - N_HEADS)).reshape(1, LANES)
    wt = jnp.concatenate([w_in[:, :A_WIDTH].T, w_in[:, 2 * A_WIDTH:a3].T], axis=0).astype(BF16)
    gqp = jnp.pad(q_norm_g, (0, LANES - D_HEAD)).reshape(1, LANES)
    gq_col = q_norm_g.reshape(D_HEAD, 1)
    gkp = jnp.pad(k_norm_g, (0, LANES - D_HEAD)).reshape(1, LANES)
    heads = jnp.arange(N_HEADS)
    e2k = jnp.zeros((LANES, QA_WIDTH), F32)
    cq = jnp.zeros((1, QA_WIDTH), F32)
    for piece in range(N_SPLIT):
        e2k = e2k.at[piece * N_HEADS + heads, HEAD_PAD * heads + D_HEAD + piece].set(-1.0)
        cq = cq.at[0, HEAD_PAD * heads + D_HEAD + piece].set(1.0)
    e2k = e2k.astype(BF16)
    gmix = norm_mix_g.reshape(1, D_MODEL)
    gffn = norm_ffn_g.reshape(1, D_MODEL)
    post_w = (w_glu.astype(BF16), b_glu.reshape(1, S_WIDTH), w_up_a.astype(BF16), w_up_s.astype(BF16),
              w_out.astype(BF16), w_ffn_up.astype(BF16), w_ffn_down.astype(BF16))

    mod = _adaln(jnp.concatenate([c_prompt, c_sample], axis=0), w_ada.astype(BF16), b_ada)
    mod = mod.reshape(nbp + nbs, 1, 6, D_MODEL)
    mods_p = [mod[:nbp, :, i, :] for i in range(6)]
    mods_s = [mod[nbp:, :, i, :] for i in range(6)]

    ab_re, ab_im, bb_re, bb_im = _s5_disc(ssm_lambda_re, ssm_lambda_im, ssm_log_dt, ssm_b_re, ssm_b_im)
    a_row = jnp.concatenate([ab_re.reshape(1, N_STATE), ab_im.reshape(1, N_STATE)], axis=1)
    eye_g = jnp.eye(N_GROUPS, dtype=F32)

    def b_blockdiag(bb):
        return jnp.einsum('gcp,gh->gchp', bb, eye_g).reshape(S_WIDTH, N_STATE)

    def c_blockdiag(cc):
        return jnp.einsum('gcp,gh->gphc', cc, eye_g).reshape(N_STATE, S_WIDTH)

    bmat = jnp.concatenate([b_blockdiag(bb_re), b_blockdiag(bb_im)], axis=1).astype(BF16)
    cmat = jnp.concatenate([c_blockdiag(ssm_c_re), -c_blockdiag(ssm_c_im)], axis=0).astype(BF16)
    d_row = ssm_d.reshape(1, S_WIDTH)

    tt = min(512, ntp)
    ablk = min(512, ntp)
    bbs = min(64, nbs)
    qa_s, ka_s, k_s, v_s, vb_s, lf_s, u_s, sga_s, sgs_s, qn_s = _inproj(
        x_sample, mods_s[0], mods_s[1], gmix, w2, wt, bfp, gqp, gq_col, gkp, e2k, cq,
        bb=bbs, tt=nts, prompt=False, kblk=ablk)
    lf_rows = jnp.swapaxes(cache_logf, 1, 2).reshape(n_pool * N_HEADS, PAGE)
    rsuf = _pool_logf(lf_rows).reshape(n_pool, N_HEADS, 2 * PAGE)
    cache_kt = jnp.transpose(cache_k, (0, 2, 3, 1)).reshape(n_pool, A_WIDTH, PAGE)
    cache_vt = jnp.transpose(cache_v, (0, 2, 3, 1)).reshape(n_pool, A_WIDTH, PAGE)
    sattn_args = (page_table, _block_diag_heads(qn_s, D_HEAD), _block_diag_heads(qa_s, HEAD_PAD),
                  ka_s, vb_s, cache_kt, cache_vt, rsuf)

    qat, ka, k_p, vt_p, vtb, lf_p, u_tm, sga, sgs = _inproj(
        x_prompt, mods_p[0], mods_p[1], gmix, w2, wt, bfp, gqp, gq_col, gkp, e2k, cq,
        bb=1, tt=tt, prompt=True, kblk=ablk)
    attn_p = _attn(qat, ka, vtb, qblk=min(512, ntp), kblk=ablk)
    v_p = jnp.transpose(vt_p.reshape(nbp, N_HEADS, D_HEAD, ntp), (0, 3, 1, 2))
    h0 = jnp.zeros((nbp, 2 * N_STATE), F32)
    ys_tm, ht_p = _s5(u_tm.reshape(ntp, nbp, S_WIDTH), h0, a_row, bmat, cmat, d_row, tc=min(64, ntp))
    y_prompt, attn_s = _post(x_prompt, attn_p, ys_tm.reshape(ntp, nbp * S_WIDTH), sga, sgs,
                             mods_p[2], mods_p[3], mods_p[4], mods_p[5], gffn, *post_w,
                             bb=1, tt=tt, ys_time_major=True, sattn_args=sattn_args)

    h0_s = jnp.concatenate([state_ssm_re.reshape(nbs, N_STATE), state_ssm_im.reshape(nbs, N_STATE)], axis=1)
    ys_s_tm, ht_s = _s5(jnp.swapaxes(u_s, 0, 1), h0_s, a_row, bmat, cmat, d_row, tc=nts)
    y_sample = _post(x_sample, attn_s, jnp.swapaxes(ys_s_tm, 0, 1), sga_s, sgs_s,
                     mods_s[2], mods_s[3], mods_s[4], mods_s[5], gffn, *post_w,
                     bb=bbs, tt=nts, ys_time_major=False)

    def heads4(z):
        return z.reshape(z.shape[0], z.shape[1], N_HEADS, D_HEAD)

    def state3(z):
        return z.reshape(z.shape[0], N_GROUPS, STATE_P)

    return (y_prompt, y_sample, heads4(k_p), heads4(v_p), lf_p,
            state3(ht_p[:, :N_STATE]), state3(ht_p[:, N_STATE:]),
            heads4(k_s), heads4(v_s), lf_s,
            state3(ht_s[:, :N_STATE]), state3(ht_s[:, N_STATE:]))
```

```python
import functools
import math

import jax
import jax.numpy as jnp
from jax import lax
from jax.experimental import pallas as pl
from jax.experimental.pallas import tpu as pltpu

F32 = jnp.float32
BF16 = jnp.bfloat16

D_MODEL = 1024
N_HEADS = 8
D_HEAD = 64
A_WIDTH = N_HEADS * D_HEAD
S_WIDTH = D_MODEL // 2
GROUP_CH = 16
N_GROUPS = S_WIDTH // GROUP_CH
STATE_P = 64
N_STATE = N_GROUPS * STATE_P
D_FF = 4 * D_MODEL
PAGE = 128
ATTN_SCALE = 1.0 / math.sqrt(D_HEAD)
LOG2_E = math.log2(math.e)
NORM_EPS = 1e-6
NEG_BIG = -1e30

LANES = 128
MXU_TILE = 256
HEAD_PAD = 128
QA_WIDTH = N_HEADS * HEAD_PAD
N_SPLIT = 3
VMEM_LIMIT = 56 * 1024 * 1024
POST_VMEM_LIMIT = 60 * 1024 * 1024
PAGE_SLOTS = 2

_OQ, _OK, _OV, _OU, _OGA, _OGS, _OF, _W2 = 0, 512, 1024, 1536, 2048, 3072, 4096, 4224


def _cparams(*sem):
    return pltpu.CompilerParams(dimension_semantics=sem, vmem_limit_bytes=VMEM_LIMIT)


def _const_spec(shape):
    nd = len(shape)
    return pl.BlockSpec(shape, lambda *_: (0,) * nd, pipeline_mode=pl.Buffered(1))


def _split3(x):
    hi = x.astype(BF16)
    r1 = x - hi.astype(F32)
    mid = r1.astype(BF16)
    lo = (r1 - mid.astype(F32)).astype(BF16)
    return hi, mid, lo


def _dot(a, b):
    return jnp.dot(a, b, preferred_element_type=F32)


def _dot_nt(a, b):
    return lax.dot_general(a, b, (((1,), (1,)), ((), ())), preferred_element_type=F32)


def _adaln_kernel(c_ref, w_ref, b_ref, o_ref):
    c = c_ref[...]
    s = (c * jax.nn.sigmoid(c)).astype(BF16)
    o_ref[...] = _dot(s, w_ref[...]) + b_ref[...]


def _adaln(c_all, w_ada_bf, b_ada):
    n = c_all.shape[0]
    tn = 1024
    return pl.pallas_call(
        _adaln_kernel,
        grid=(6 * D_MODEL // tn,),
        in_specs=[pl.BlockSpec((n, D_MODEL), lambda j: (0, 0)),
                  pl.BlockSpec((D_MODEL, tn), lambda j: (0, j)),
                  pl.BlockSpec((1, tn), lambda j: (0, j))],
        out_specs=pl.BlockSpec((n, tn), lambda j: (0, j)),
        out_shape=jax.ShapeDtypeStruct((n, 6 * D_MODEL), F32),
        compiler_params=_cparams("parallel"),
        name="adaln",
    )(c_all, w_ada_bf, b_ada.reshape(1, -1))


def _s5_disc_kernel(lr_ref, li_ref, ldt_ref, br_ref, bi_ref, abr_ref, abi_ref, bbr_ref, bbi_ref):
    lr = lr_ref[...]
    li = li_ref[...]
    dt = jnp.exp(ldt_ref[...])
    mag = jnp.exp(lr * dt)
    ab_re = mag * jnp.cos(li * dt)
    ab_im = mag * jnp.sin(li * dt)
    nr, ni = ab_re - 1.0, ab_im
    den = lr * lr + li * li
    f_re = (nr * lr + ni * li) / den
    f_im = (ni * lr - nr * li) / den
    br = br_ref[...]
    bi = bi_ref[...]
    abr_ref[...] = ab_re
    abi_ref[...] = ab_im
    bbr_ref[...] = f_re * br - f_im * bi
    bbi_ref[...] = f_re * bi + f_im * br


def _s5_disc(lam_re, lam_im, log_dt, b_re, b_im):
    g3 = jax.ShapeDtypeStruct((N_GROUPS, 1, STATE_P), F32)
    b3 = jax.ShapeDtypeStruct((N_GROUPS, GROUP_CH, STATE_P), F32)
    return pl.pallas_call(
        _s5_disc_kernel,
        out_shape=(g3, g3, b3, b3),
        name="s5_disc",
    )(lam_re.reshape(N_GROUPS, 1, STATE_P), lam_im.reshape(N_GROUPS, 1, STATE_P),
      log_dt.reshape(N_GROUPS, 1, 1),
      jnp.swapaxes(b_re, 1, 2), jnp.swapaxes(b_im, 1, 2))


def _inproj_kernel(x_ref, sh_ref, sc_ref, gmix_ref, w_ref, wt_ref, bf_ref, gq_ref, gqcol_ref, gk_ref,
                   e2k_ref, cq_ref, tri_ref, *rest, bb, tt, prompt, kblk):
    if prompt:
        qat_ref, ka_ref, k_ref, vt_ref, vtb_ref, lf_ref, u_ref, sga_ref, sgs_ref, carry_ref = rest
    else:
        qa_ref, ka_ref, k_ref, v_ref, vb_ref, lf_ref, u_ref, sga_ref, sgs_ref, qn_ref, carry_ref = rest
    rows = bb * tt
    t = pl.program_id(1)

    x = x_ref[...]
    ms = jnp.mean(x * x, axis=-1, keepdims=True)
    h = x * lax.rsqrt(ms + NORM_EPS) * gmix_ref[...]
    h = h * (1.0 + sc_ref[...]) + sh_ref[...]
    hb = h.reshape(rows, D_MODEL).astype(BF16)

    lane = lax.broadcasted_iota(jnp.int32, (1, LANES), 1)
    low_half = lane < D_HEAD

    zf = _dot(hb, w_ref[:, _OF:_W2]) + bf_ref[...]
    lf = jnp.minimum(zf, 0.0) - jnp.log1p(jnp.exp(-jnp.abs(zf)))
    lf = jnp.where(lane < N_HEADS, lf, 0.0)
    lf_ref[...] = lf[:, :N_HEADS].reshape(lf_ref.shape)
    hi, mid, lo = _split3(lf)
    packed = (hi.astype(F32) + pltpu.roll(mid.astype(F32), N_HEADS, 1)
              + pltpu.roll(lo.astype(F32), 2 * N_HEADS, 1)).astype(BF16)
    gc = _dot(tri_ref[...], packed)
    g = gc + pltpu.roll(gc, LANES - N_HEADS, 1) + pltpu.roll(gc, LANES - 2 * N_HEADS, 1)
    g = jnp.where(lane < N_HEADS, g, 0.0)
    if bb == 1:
        @pl.when(t == 0)
        def _():
            carry_ref[...] = jnp.zeros_like(carry_ref)
        g = g + carry_ref[...]
        carry_ref[...] = g[rows - 1:rows, :]
    logit_scale = LOG2_E if prompt else 1.0
    ghi, gmid, glo = _split3(g * logit_scale)
    gpacked = (ghi.astype(F32) + pltpu.roll(gmid.astype(F32), N_HEADS, 1)
               + pltpu.roll(glo.astype(F32), 2 * N_HEADS, 1)).astype(BF16)
    kaug = _dot(gpacked, e2k_ref[...])

    def widen_norm(z, gain):
        out = []
        for part in (z, pltpu.roll(z, D_HEAD, 1)):
            e = jnp.where(low_half, part, 0.0)
            ss = jnp.sum(e * e, axis=-1, keepdims=True)
            out.append(e * lax.rsqrt(ss * (1.0 / D_HEAD) + NORM_EPS) * gain)
        return out

    gq = gq_ref[...] * ATTN_SCALE
    gk = gk_ref[...]
    for j in range(N_HEADS // 2):
        zk = _dot(hb, w_ref[:, _OK + LANES * j:_OK + LANES * (j + 1)])
        k_even, k_odd = widen_norm(zk, gk)
        for i, kh in enumerate((k_even, k_odd)):
            c0 = HEAD_PAD * (2 * j + i)
            ka_ref[..., c0:c0 + HEAD_PAD] = (kh + kaug[:, c0:c0 + HEAD_PAD]).astype(BF16).reshape(bb, tt, HEAD_PAD)
        k_ref[..., LANES * j:LANES * (j + 1)] = (k_even + pltpu.roll(k_odd, D_HEAD, 1)).reshape(bb, tt, LANES)
        if not prompt:
            zq = _dot(hb, w_ref[:, _OQ + LANES * j:_OQ + LANES * (j + 1)])
            q_even, q_odd = widen_norm(zq, gq)
            for i, qh in enumerate((q_even, q_odd)):
                c0 = HEAD_PAD * (2 * j + i)
                qa_ref[..., c0:c0 + HEAD_PAD] = (
                    (qh + cq_ref[:, c0:c0 + HEAD_PAD]).astype(BF16).reshape(bb, tt, HEAD_PAD))
            qn_ref[..., LANES * j:LANES * (j + 1)] = (
                (q_even + pltpu.roll(q_odd, D_HEAD, 1)).astype(BF16).reshape(bb, tt, LANES))

    if prompt:
        zqt = _dot_nt(wt_ref[:A_WIDTH, :], hb)
        gq_col = gqcol_ref[...] * (ATTN_SCALE * logit_scale)
        sub = lax.broadcasted_iota(jnp.int32, (D_HEAD, tt), 0)
        ones_rows = jnp.where(sub < N_SPLIT, 1.0, 0.0).astype(BF16)
        for hd in range(N_HEADS):
            z = zqt[D_HEAD * hd:D_HEAD * (hd + 1), :]
            ss = jnp.sum(z * z, axis=0, keepdims=True)
            qn_t = z * lax.rsqrt(ss * (1.0 / D_HEAD) + NORM_EPS) * gq_col
            qat_ref[0, HEAD_PAD * hd:HEAD_PAD * hd + D_HEAD, :] = qn_t.astype(BF16)
            qat_ref[0, HEAD_PAD * hd + D_HEAD:HEAD_PAD * (hd + 1), :] = ones_rows
        zvt = _dot_nt(wt_ref[A_WIDTH:, :], hb)
        vt_ref[0] = zvt
        for c in range(tt // kblk):
            vtb_ref[0, c] = zvt[:, kblk * c:kblk * (c + 1)].astype(BF16)
    else:
        zv = _dot(hb, w_ref[:, _OV:_OU])
        v_ref[...] = zv.reshape(v_ref.shape)
        vb_ref[...] = zv.astype(BF16).reshape(vb_ref.shape)
    u_ref[...] = _dot(hb, w_ref[:, _OU:_OGA]).reshape(u_ref.shape)
    sga_ref[...] = jax.nn.sigmoid(_dot(hb, w_ref[:, _OGA:_OGS])).astype(BF16).reshape(sga_ref.shape)
    sgs_ref[...] = jax.nn.sigmoid(_dot(hb, w_ref[:, _OGS:_OF])).astype(BF16).reshape(sgs_ref.shape)


def _inproj(x, sh1, sc1, gmix, w2, wt, bfp, gqp, gq_col, gkp, e2k, cq, *, bb, tt, prompt, kblk):
    nb, nt_total, _ = x.shape
    u_time_major = prompt
    rows = bb * tt
    idx = jnp.arange(rows)
    tri = ((idx[:, None] >= idx[None, :]) & (idx[:, None] // tt == idx[None, :] // tt)).astype(BF16)
    grid = (nb // bb, nt_total // tt)
    assert bb == 1 or grid[1] == 1

    def tok(width):
        return pl.BlockSpec((bb, tt, width), lambda b, t: (b, t, 0))

    def mod():
        return pl.BlockSpec((bb, 1, D_MODEL), lambda b, t: (b, 0, 0))

    if u_time_major:
        assert bb == 1
        u_shape = jax.ShapeDtypeStruct((nt_total, nb * S_WIDTH), F32)
        u_spec = pl.BlockSpec((tt, S_WIDTH), lambda b, t: (t, b))
    else:
        u_shape = jax.ShapeDtypeStruct((nb, nt_total, S_WIDTH), F32)
        u_spec = tok(S_WIDTH)

    def sds(width, dt):
        return jax.ShapeDtypeStruct((nb, nt_total, width), dt)

    tail_shape = [sds(N_HEADS, F32), u_shape, sds(D_MODEL, BF16), sds(D_MODEL, BF16)]
    tail_specs = [tok(N_HEADS), u_spec, tok(D_MODEL), tok(D_MODEL)]
    if prompt:
        assert tt % kblk == 0
        out_shape = [jax.ShapeDtypeStruct((nb, QA_WIDTH, nt_total), BF16), sds(QA_WIDTH, BF16), sds(A_WIDTH, F32),
                     jax.ShapeDtypeStruct((nb, A_WIDTH, nt_total), F32),
                     jax.ShapeDtypeStruct((nb, nt_total // kblk, A_WIDTH, kblk), BF16)] + tail_shape
        out_specs = [pl.BlockSpec((1, QA_WIDTH, tt), lambda b, t: (b, 0, t)), tok(QA_WIDTH), tok(A_WIDTH),
                     pl.BlockSpec((1, A_WIDTH, tt), lambda b, t: (b, 0, t)),
                     pl.BlockSpec((1, tt // kblk, A_WIDTH, kblk), lambda b, t: (b, t, 0, 0))] + tail_specs
    else:
        out_shape = ([sds(QA_WIDTH, BF16), sds(QA_WIDTH, BF16), sds(A_WIDTH, F32), sds(A_WIDTH, F32),
                      sds(A_WIDTH, BF16)] + tail_shape + [sds(A_WIDTH, BF16)])
        out_specs = ([tok(QA_WIDTH), tok(QA_WIDTH), tok(A_WIDTH), tok(A_WIDTH), tok(A_WIDTH)]
                     + tail_specs + [tok(A_WIDTH)])

    return pl.pallas_call(
        functools.partial(_inproj_kernel, bb=bb, tt=tt, prompt=prompt, kblk=kblk),
        grid=grid,
        in_specs=[tok(D_MODEL), mod(), mod(), _const_spec((1, D_MODEL)), _const_spec((D_MODEL, _W2)),
                  _const_spec((2 * A_WIDTH, D_MODEL)),
                  _const_spec((1, LANES)), _const_spec((1, LANES)), _const_spec((D_HEAD, 1)), _const_spec((1, LANES)),
                  _const_spec((LANES, QA_WIDTH)), _const_spec((1, QA_WIDTH)), _const_spec((rows, rows))],
        out_specs=out_specs,
        out_shape=out_shape,
        scratch_shapes=[pltpu.VMEM((1, LANES), F32)],
        compiler_params=_cparams("parallel", "arbitrary"),
        name="inproj",
    )(x, sh1, sc1, gmix, w2, wt, bfp, gqp, gq_col, gkp, e2k, cq, tri)


def _attn_kernel(qat_ref, ka_ref, vtb_ref, o_ref, m_sc, l_sc, acc_sc, *, qblk, kblk):
    qi = pl.program_id(1)
    ratio = qblk // kblk
    m_sc[...] = jnp.full_like(m_sc, NEG_BIG)
    l_sc[...] = jnp.zeros_like(l_sc)
    acc_sc[...] = jnp.zeros_like(acc_sc)
    key = lax.broadcasted_iota(jnp.int32, (kblk, qblk), 0)
    qry = lax.broadcasted_iota(jnp.int32, (kblk, qblk), 1)

    def block(kb, diag):
        ks = pl.multiple_of(kb * kblk, kblk)
        for h in range(N_HEADS):
            k = ka_ref[0, pl.ds(ks, kblk), HEAD_PAD * h:HEAD_PAD * (h + 1)]
            st = _dot(k, qat_ref[0, HEAD_PAD * h:HEAD_PAD * (h + 1), :])
            if diag is not None:
                st = jnp.where(key + diag * kblk <= qry, st, NEG_BIG)
            m_old = m_sc[h]
            m_new = jnp.maximum(m_old, jnp.max(st, axis=0, keepdims=True))
            alpha = jnp.exp2(m_old - m_new)
            p = jnp.exp2(st - m_new)
            l_sc[h] = alpha * l_sc[h] + jnp.sum(p, axis=0, keepdims=True)
            vt = vtb_ref[0, kb, D_HEAD * h:D_HEAD * (h + 1), :]
            rows = slice(D_HEAD * h, D_HEAD * (h + 1))
            acc_sc[rows, :] = alpha * acc_sc[rows, :] + _dot(vt, p.astype(BF16))
            m_sc[h] = m_new

    def body(kb, carry):
        block(kb, None)
        return carry

    lax.fori_loop(0, qi * ratio, body, 0)
    for d in range(ratio):
        block(qi * ratio + d, d)

    for hp in range(N_HEADS // 2):
        inv = jnp.concatenate([jnp.broadcast_to(1.0 / l_sc[2 * hp + i], (D_HEAD, qblk)) for i in range(2)], axis=0)
        pair = acc_sc[LANES * hp:LANES * (hp + 1), :] * inv
        o_ref[0, :, LANES * hp:LANES * (hp + 1)] = pair.T.astype(BF16)


def _attn(qat, ka, vtb, *, qblk, kblk):
    nb, nt, _ = ka.shape
    assert vtb.shape == (nb, nt // kblk, A_WIDTH, kblk) and qblk % kblk == 0
    return pl.pallas_call(
        functools.partial(_attn_kernel, qblk=qblk, kblk=kblk),
        grid=(nb, nt // qblk),
        in_specs=[pl.BlockSpec((1, QA_WIDTH, qblk), lambda b, i: (b, 0, i)),
                  pl.BlockSpec((1, nt, QA_WIDTH), lambda b, i: (b, 0, 0)),
                  pl.BlockSpec((1, nt // kblk, A_WIDTH, kblk), lambda b, i: (b, 0, 0, 0))],
        out_specs=pl.BlockSpec((1, qblk, A_WIDTH), lambda b, i: (b, i, 0)),
        out_shape=jax.ShapeDtypeStruct((nb, nt, A_WIDTH), BF16),
        scratch_shapes=[pltpu.VMEM((N_HEADS, 1, qblk), F32), pltpu.VMEM((N_HEADS, 1, qblk), F32),
                        pltpu.VMEM((A_WIDTH, qblk), F32)],
        compiler_params=_cparams("parallel", "arbitrary"),
        name="attn",
    )(qat, ka, vtb)


def _pool_logf_kernel(lf_ref, m_ref, o_ref):
    hi, mid, lo = _split3(lf_ref[...])
    m = m_ref[...]
    o_ref[...] = _dot(hi, m) + _dot(mid, m) + _dot(lo, m)


def _pool_logf(lf_rows):
    n_rows = lf_rows.shape[0]
    tile = next(c for c in (2048, 1024, 512, 256, 128, 64, 32, 16, 8) if n_rows % c == 0)
    j_src = jnp.arange(PAGE)
    suffix = j_src[:, None] > j_src[None, :]
    m = jnp.concatenate([suffix, jnp.ones((PAGE, PAGE), bool)], axis=1).astype(BF16)
    return pl.pallas_call(
        _pool_logf_kernel,
        grid=(n_rows // tile,),
        in_specs=[pl.BlockSpec((tile, PAGE), lambda i: (i, 0)), _const_spec((PAGE, 2 * PAGE))],
        out_specs=pl.BlockSpec((tile, 2 * PAGE), lambda i: (i, 0)),
        out_shape=jax.ShapeDtypeStruct((n_rows, 2 * PAGE), F32),
        compiler_params=_cparams("parallel"),
        name="pool_logf",
    )(lf_rows, m)


def _sattn_chunk(g, idx, refs, *, pp, ts, n_pages, cps):
    (pt_ref, qbd_ref, qabd_ref, kan_ref, vn_ref, k_hbm, v_hbm, r_hbm, o_ref,
     kpage, vpage, rpage, sem, m_sc, l_sc, acc_sc, tot_sc) = refs
    cpq = n_pages // pp
    spp = cps // cpq
    n_steps = pl.num_programs(0) * pl.num_programs(1)
    nrow = N_HEADS * ts

    def page_copies(ahead):
        g_, idx_ = g + (idx + ahead) // cps, (idx + ahead) % cps
        seq, chunk = g_ * spp + idx_ // cpq, idx_ % cpq
        slot_ = lax.rem(g_ * cps + idx_, PAGE_SLOTS)
        out = []
        for i in range(pp):
            pid = pt_ref[seq * n_pages + (n_pages - 1 - (chunk * pp + i))]
            lanes = pl.ds(PAGE * i, PAGE)
            out.append(pltpu.make_async_copy(k_hbm.at[pid], kpage.at[slot_, :, lanes], sem.at[0, slot_]))
            out.append(pltpu.make_async_copy(v_hbm.at[pid], vpage.at[slot_, :, lanes], sem.at[1, slot_]))
            out.append(pltpu.make_async_copy(r_hbm.at[pid], rpage.at[slot_, i], sem.at[2, slot_]))
        return out

    depth = PAGE_SLOTS - 1
    if idx == 0:
        @pl.when(g == 0)
        def _():
            for ahead in range(depth):
                for cp in page_copies(ahead):
                    cp.start()
    if idx + depth < cps:
        for cp in page_copies(depth):
            cp.start()
    else:
        @pl.when(g + 1 < n_steps)
        def _():
            for cp in page_copies(depth):
                cp.start()
    for cp in page_copies(0):
        cp.wait()

    slot = lax.rem(g * cps + idx, PAGE_SLOTS)
    seq_l, chunk = idx // cpq, idx % cpq
    if chunk == 0:
        m_sc[...] = jnp.full_like(m_sc, NEG_BIG)
        l_sc[...] = jnp.zeros_like(l_sc)
        acc_sc[...] = jnp.zeros_like(acc_sc)
        tot_sc[...] = jnp.zeros_like(tot_sc)

    def update(s, pv):
        m_old = m_sc[...]
        m_new = jnp.maximum(m_old, jnp.max(s, axis=-1, keepdims=True))
        alpha = jnp.exp(m_old - m_new)
        p = jnp.exp(s - m_new)
        l_sc[...] = alpha * l_sc[...] + jnp.sum(p, axis=-1, keepdims=True)
        acc_sc[...] = alpha * acc_sc[...] + pv(p)
        m_sc[...] = m_new

    later = tot_sc[...]
    biases = []
    for i in range(pp):
        r = rpage[slot, i]
        biases.append(jnp.concatenate([r[:, :PAGE] + later] * ts, axis=0))
        later = later + r[:, PAGE:]
    tot_sc[...] = later
    s = _dot(qbd_ref[seq_l].astype(F32), kpage[slot]) + jnp.concatenate(biases, axis=1)
    update(s, lambda p: _dot_nt(p, vpage[slot]))

    if chunk == cpq - 1:
        s_new = _dot_nt(qabd_ref[seq_l], kan_ref[seq_l])
        r_i = lax.broadcasted_iota(jnp.int32, (nrow, ts), 0)
        c_i = lax.broadcasted_iota(jnp.int32, (nrow, ts), 1)
        update(jnp.where((r_i // N_HEADS) >= c_i, s_new, NEG_BIG), lambda p: _dot(p.astype(BF16), vn_ref[seq_l]))
        o = acc_sc[...] / l_sc[...]
        lane_head = lax.broadcasted_iota(jnp.int32, (N_HEADS, A_WIDTH), 1) // D_HEAD
        own = lane_head == lax.broadcasted_iota(jnp.int32, (N_HEADS, A_WIDTH), 0)
        rows = [jnp.sum(jnp.where(own, o[N_HEADS * t:N_HEADS * (t + 1), :], 0.0), axis=0, keepdims=True)
                for t in range(ts)]
        o_ref[seq_l] = jnp.concatenate(rows, axis=0).astype(BF16)


def _gelu_tanh(x):
    return 0.5 * x * (1.0 + jnp.tanh(math.sqrt(2.0 / math.pi) * (x + 0.044715 * (x * x * x))))


def _s5_kernel(u_ref, h0_ref, a_ref, bmat_ref, cmat_ref, d_ref, y_ref, ht_ref, hist, state,
               *, tc, bs, rb, lc):
    t = pl.program_id(0)

    @pl.when(t == 0)
    def _():
        state[...] = h0_ref[...]

    u = u_ref[...].reshape(tc * bs, S_WIDTH)
    ub = u.astype(BF16)
    for n in range(2 * N_STATE // MXU_TILE):
        slab = LANES * ((n % (N_STATE // MXU_TILE)) // 2)
        cols = slice(MXU_TILE * n, MXU_TILE * (n + 1))
        hist[:, cols] = _dot(ub[:, slab:slab + LANES], bmat_ref[slab:slab + LANES, cols])

    for c in range(N_STATE // lc):
        re = slice(lc * c, lc * (c + 1))
        im = slice(N_STATE + lc * c, N_STATE + lc * (c + 1))
        a_re = jnp.broadcast_to(a_ref[:, re], (rb, lc))
        a_im = jnp.broadcast_to(a_ref[:, im], (rb, lc))

        def row_block(r, _, re=re, im=im, a_re=a_re, a_im=a_im):
            r0 = pl.multiple_of(r * rb, rb)

            def step(j, carry):
                h_re, h_im = carry
                row = pl.multiple_of(j * bs + r0, rb)
                n_re = a_re * h_re - a_im * h_im + hist[pl.ds(row, rb), re]
                n_im = a_re * h_im + a_im * h_re + hist[pl.ds(row, rb), im]
                hist[pl.ds(row, rb), re] = n_re
                hist[pl.ds(row, rb), im] = n_im
                return n_re, n_im

            h_re, h_im = lax.fori_loop(0, tc, step, (state[pl.ds(r0, rb), re], state[pl.ds(r0, rb), im]))
            state[pl.ds(r0, rb), re] = h_re
            state[pl.ds(r0, rb), im] = h_im
            return 0

        lax.fori_loop(0, bs // rb, row_block, 0)

    half = N_STATE // (S_WIDTH // MXU_TILE)
    ys = []
    for m_ in range(S_WIDTH // MXU_TILE):
        cols = slice(MXU_TILE * m_, MXU_TILE * (m_ + 1))
        re = slice(half * m_, half * (m_ + 1))
        im = slice(N_STATE + half * m_, N_STATE + half * (m_ + 1))
        ys.append(_dot(hist[:, re].astype(BF16), cmat_ref[re, cols])
                  + _dot(hist[:, im].astype(BF16), cmat_ref[im, cols]))
    y = jnp.concatenate(ys, axis=1) + d_ref[...] * u
    y_ref[...] = _gelu_tanh(y).astype(BF16).reshape(y_ref.shape)
    ht_ref[...] = state[...]


def _s5(u_tm, h0, a_row, bmat, cmat, d_row, *, tc):
    nt, bs, _ = u_tm.shape
    rb = 16 if bs % 16 == 0 else 8
    assert bs % rb == 0 and nt % tc == 0
    return pl.pallas_call(
        functools.partial(_s5_kernel, tc=tc, bs=bs, rb=rb, lc=512),
        grid=(nt // tc,),
        in_specs=[pl.BlockSpec((tc, bs, S_WIDTH), lambda t: (t, 0, 0)),
                  _const_spec((bs, 2 * N_STATE)), _const_spec((1, 2 * N_STATE)),
                  _const_spec((S_WIDTH, 2 * N_STATE)), _const_spec((2 * N_STATE, S_WIDTH)),
                  _const_spec((1, S_WIDTH))],
        out_specs=[pl.BlockSpec((tc, bs, S_WIDTH), lambda t: (t, 0, 0)),
                   pl.BlockSpec((bs, 2 * N_STATE), lambda t: (0, 0))],
        out_shape=[jax.ShapeDtypeStruct((nt, bs, S_WIDTH), BF16),
                   jax.ShapeDtypeStruct((bs, 2 * N_STATE), F32)],
        scratch_shapes=[pltpu.VMEM((tc * bs, 2 * N_STATE), F32), pltpu.VMEM((bs, 2 * N_STATE), F32)],
        compiler_params=_cparams("arbitrary"),
        name="s5",
    )(u_tm, h0, a_row, bmat, cmat, d_row)


SCAN_BLOCK = 16


def _s5p_kernel(u_ref, h0_ref, a_ref, bmat_ref, cmat_ref, d_ref, y_ref, ht_ref,
                hist_a, hist_b, du_a, du_b, yacc, state, *, tc, bs, lc):
    t = pl.program_id(0)
    n_real = pl.num_programs(0) - 1
    rows = tc * bs
    n_kt = N_STATE // MXU_TILE
    half = N_STATE // (S_WIDTH // MXU_TILE)

    @pl.when(t == 0)
    def _():
        state[...] = h0_ref[...]
        hist_b[...] = jnp.zeros_like(hist_b)
        du_b[...] = jnp.zeros_like(du_b)

    def step(cur, du_cur, prev, du_prev):
        u = u_ref[...].reshape(rows, S_WIDTH)
        ub = u.astype(BF16)
        du_cur[...] = (d_ref[...] * u).astype(BF16)
        for n in range(2 * n_kt):
            slab = LANES * ((n % n_kt) // 2)
            cols = slice(MXU_TILE * n, MXU_TILE * (n + 1))
            cur[:, cols] = _dot(ub[:, slab:slab + LANES], bmat_ref[slab:slab + LANES, cols])

        live = t < n_real
        piece = 0
        for c in range(N_STATE // lc):
            re = slice(lc * c, lc * (c + 1))
            im = slice(N_STATE + lc * c, N_STATE + lc * (c + 1))
            a_re = jnp.broadcast_to(a_ref[:, re], (bs, lc))
            a_im = jnp.broadcast_to(a_ref[:, im], (bs, lc))
            h_re, h_im = state[:, re], state[:, im]
            for q in range(tc // SCAN_BLOCK):
                for j in range(SCAN_BLOCK):
                    r = slice((q * SCAN_BLOCK + j) * bs, (q * SCAN_BLOCK + j + 1) * bs)
                    n_re = a_re * h_re - a_im * h_im + cur[r, re]
                    n_im = a_re * h_im + a_im * h_re + cur[r, im]
                    cur[r, re] = n_re
                    cur[r, im] = n_im
                    h_re, h_im = n_re, n_im
                m_, kk = piece // (2 * n_kt // 2), piece % (2 * n_kt // 2)
                k0 = half * m_ + MXU_TILE * kk if kk < n_kt // 2 else N_STATE + half * m_ + MXU_TILE * (kk - n_kt // 2)
                krows = slice(k0, k0 + MXU_TILE)
                cols = slice(MXU_TILE * m_, MXU_TILE * (m_ + 1))
                part = _dot(prev[:, krows].astype(BF16), cmat_ref[krows, cols])
                if kk == 0:
                    yacc[:, cols] = part
                else:
                    yacc[:, cols] += part
                piece += 1
            state[:, re] = jnp.where(live, h_re, state[:, re])
            state[:, im] = jnp.where(live, h_im, state[:, im])
        assert piece == (S_WIDTH // MXU_TILE) * (2 * half // MXU_TILE)
        y_ref[...] = _gelu_tanh(yacc[...] + du_prev[...].astype(F32)).astype(BF16).reshape(y_ref.shape)

    @pl.when(t % 2 == 0)
    def _():
        step(hist_a, du_a, hist_b, du_b)

    @pl.when(t % 2 == 1)
    def _():
        step(hist_b, du_b, hist_a, du_a)

    ht_ref[...] = state[...]


def _s5_pipelined(u_tm, h0, a_row, bmat, cmat, d_row, *, tc):
    nt, bs, _ = u_tm.shape
    n_chunks = nt // tc
    assert nt % tc == 0 and tc % SCAN_BLOCK == 0 and bs % 8 == 0
    lc = 512
    assert (N_STATE // lc) * (tc // SCAN_BLOCK) == (S_WIDTH // MXU_TILE) * (2 * N_STATE // (S_WIDTH // MXU_TILE) // MXU_TILE)
    rows = tc * bs
    return pl.pallas_call(
        functools.partial(_s5p_kernel, tc=tc, bs=bs, lc=lc),
        grid=(n_chunks + 1,),
        in_specs=[pl.BlockSpec((tc, bs, S_WIDTH), lambda t: (jnp.minimum(t, n_chunks - 1), 0, 0)),
                  _const_spec((bs, 2 * N_STATE)), _const_spec((1, 2 * N_STATE)),
                  _const_spec((S_WIDTH, 2 * N_STATE)), _const_spec((2 * N_STATE, S_WIDTH)),
                  _const_spec((1, S_WIDTH))],
        out_specs=[pl.BlockSpec((tc, bs, S_WIDTH), lambda t: (jnp.maximum(t - 1, 0), 0, 0)),
                   pl.BlockSpec((bs, 2 * N_STATE), lambda t: (0, 0))],
        out_shape=[jax.ShapeDtypeStruct((nt, bs, S_WIDTH), BF16),
                   jax.ShapeDtypeStruct((bs, 2 * N_STATE), F32)],
        scratch_shapes=[pltpu.VMEM((rows, 2 * N_STATE), F32), pltpu.VMEM((rows, 2 * N_STATE), F32),
                        pltpu.VMEM((rows, S_WIDTH), BF16), pltpu.VMEM((rows, S_WIDTH), BF16),
                        pltpu.VMEM((rows, S_WIDTH), F32), pltpu.VMEM((bs, 2 * N_STATE), F32)],
        compiler_params=pltpu.CompilerParams(dimension_semantics=("arbitrary",), vmem_limit_bytes=POST_VMEM_LIMIT),
        name="s5_pipelined",
    )(u_tm, h0, a_row, bmat, cmat, d_row)


N_POST_IN = 17


def _post_kernel(*refs, bb, tt, ffc, sattn):
    if sattn is None:
        post_in, (y_ref, h2_sc, down_sc) = refs[:N_POST_IN], refs[N_POST_IN:]
    else:
        pt_ref, refs = refs[0], refs[1:]
        post_in, rest = refs[:N_POST_IN], refs[N_POST_IN:]
        sa_in, (y_ref, o_ref), (h2_sc, down_sc), sa_sc = rest[:7], rest[7:9], rest[9:11], rest[11:]
        sa_refs = (pt_ref,) + tuple(sa_in) + (o_ref,) + tuple(sa_sc)
    (x_ref, a_ref, ys_ref, sga_ref, sgs_ref, g1_ref, sh2_ref, sc2_ref, g2_ref, gffn_ref,
     wglu_ref, bglu_ref, wupa_ref, wups_ref, wout_ref, wffu_ref, wffd_ref) = post_in
    rows = bb * tt

    def mix():
        ys = ys_ref[...].reshape(rows, S_WIDTH)
        glu = _dot(ys, wglu_ref[...]) + bglu_ref[...]
        s = (ys.astype(F32) * jax.nn.sigmoid(glu)).astype(BF16)
        a = a_ref[...].reshape(rows, A_WIDTH)
        merged = (sga_ref[...].reshape(rows, D_MODEL).astype(F32) * _dot(a, wupa_ref[...])
                  + sgs_ref[...].reshape(rows, D_MODEL).astype(F32) * _dot(s, wups_ref[...]))
        x1 = x_ref[...] + g1_ref[...] * _dot(merged.astype(BF16), wout_ref[...]).reshape(bb, tt, D_MODEL)
        y_ref[...] = x1
        ms = jnp.mean(x1 * x1, axis=-1, keepdims=True)
        h2 = x1 * lax.rsqrt(ms + NORM_EPS) * gffn_ref[...]
        h2_sc[...] = (h2 * (1.0 + sc2_ref[...]) + sh2_ref[...]).reshape(rows, D_MODEL).astype(BF16)

    def ffn(c):
        up = jnp.maximum(_dot(h2_sc[...], wffu_ref[:, ffc * c:ffc * (c + 1)]), 0.0)
        part = _dot((up * up).astype(BF16), wffd_ref[ffc * c:ffc * (c + 1), :])
        if c == 0:
            down_sc[...] = part
        else:
            down_sc[...] += part

    work = [mix] + [functools.partial(ffn, c) for c in range(D_FF // ffc)]
    if sattn is None:
        for item in work:
            item()
    else:
        g = pl.program_id(0) * pl.num_programs(1) + pl.program_id(1)
        cps, done = sattn["cps"], 0
        for idx in range(cps):
            _sattn_chunk(g, idx, sa_refs, **sattn)
            upto = ((idx + 1) * len(work) + cps // 2) // cps
            for item in work[done:upto]:
                item()
            done = max(done, upto)
        assert done == len(work)
    y_ref[...] = y_ref[...] + g2_ref[...] * down_sc[...].reshape(bb, tt, D_MODEL)


def _post(x, a, ys, sga, sgs, g1, sh2, sc2, g2, gffn, wglu, bglu, wupa, wups, wout, wffu, wffd,
          *, bb, tt, ys_time_major, sattn_args=None, pp=16):
    nb, nt_total, _ = x.shape
    grid = (nb // bb, nt_total // tt)
    fused = sattn_args is not None

    def imap(f):
        return (lambda b, t, pt: f(b, t)) if fused else f

    def tok(width):
        return pl.BlockSpec((bb, tt, width), imap(lambda b, t: (b, t, 0)))

    def mod():
        return pl.BlockSpec((bb, 1, D_MODEL), imap(lambda b, t: (b, 0, 0)))

    def const(shape):
        nd = len(shape)
        return pl.BlockSpec(shape, imap(lambda b, t: (0,) * nd), pipeline_mode=pl.Buffered(1))

    if ys_time_major:
        assert bb == 1
        ys_spec = pl.BlockSpec((tt, S_WIDTH), imap(lambda b, t: (t, b)))
    else:
        ys_spec = tok(S_WIDTH)

    in_specs = [tok(D_MODEL), tok(A_WIDTH), ys_spec, tok(D_MODEL), tok(D_MODEL),
                mod(), mod(), mod(), mod(), const((1, D_MODEL)),
                const((S_WIDTH, S_WIDTH)), const((1, S_WIDTH)),
                const((A_WIDTH, D_MODEL)), const((S_WIDTH, D_MODEL)),
                const((D_MODEL, D_MODEL)), const((D_MODEL, D_FF)), const((D_FF, D_MODEL))]
    assert len(in_specs) == N_POST_IN
    operands = [x, a, ys, sga, sgs, g1, sh2, sc2, g2, gffn, wglu, bglu, wupa, wups, wout, wffu, wffd]
    out_specs = tok(D_MODEL)
    out_shape = jax.ShapeDtypeStruct((nb, nt_total, D_MODEL), F32)
    rows = bb * tt
    scratch = [pltpu.VMEM((rows, D_MODEL), BF16), pltpu.VMEM((rows, D_MODEL), F32)]
    static = None
    semantics = ("parallel", "parallel")
    if fused:
        page_table, qbd, qabd, kan, vn, cache_kt, cache_vt, rsuf = sattn_args
        nbs, n_pages = page_table.shape
        ts = kan.shape[1]
        nrow = N_HEADS * ts
        n_steps = grid[0] * grid[1]
        assert nbs % n_steps == 0 and n_pages % pp == 0
        spp = nbs // n_steps
        cps = spp * (n_pages // pp)
        assert cps >= PAGE_SLOTS
        static = dict(pp=pp, ts=ts, n_pages=n_pages, cps=cps)

        def per_step(shape):
            return pl.BlockSpec(shape, lambda b, t, pt: (b * grid[1] + t, 0, 0))

        hbm = pl.BlockSpec(memory_space=pl.ANY)
        in_specs += [per_step((spp, nrow, A_WIDTH)), per_step((spp, nrow, QA_WIDTH)),
                     per_step((spp, ts, QA_WIDTH)), per_step((spp, ts, A_WIDTH)), hbm, hbm, hbm]
        operands = [page_table.reshape(-1)] + operands + [qbd, qabd, kan, vn, cache_kt, cache_vt, rsuf]
        out_specs = [out_specs, per_step((spp, ts, A_WIDTH))]
        out_shape = [out_shape, jax.ShapeDtypeStruct((nbs, ts, A_WIDTH), BF16)]
        scratch += [pltpu.VMEM((PAGE_SLOTS, A_WIDTH, pp * PAGE), F32), pltpu.VMEM((PAGE_SLOTS, A_WIDTH, pp * PAGE), F32),
                    pltpu.VMEM((PAGE_SLOTS, pp, N_HEADS, 2 * PAGE), F32), pltpu.SemaphoreType.DMA((3, PAGE_SLOTS)),
                    pltpu.VMEM((nrow, 1), F32), pltpu.VMEM((nrow, 1), F32),
                    pltpu.VMEM((nrow, A_WIDTH), F32), pltpu.VMEM((N_HEADS, PAGE), F32)]
        semantics = ("arbitrary", "arbitrary")

    grid_spec = pltpu.PrefetchScalarGridSpec(
        num_scalar_prefetch=1 if fused else 0, grid=grid, in_specs=in_specs, out_specs=out_specs,
        scratch_shapes=scratch)
    return pl.pallas_call(
        functools.partial(_post_kernel, bb=bb, tt=tt, ffc=512, sattn=static),
        grid_spec=grid_spec,
        out_shape=out_shape,
        compiler_params=pltpu.CompilerParams(dimension_semantics=semantics, vmem_limit_bytes=POST_VMEM_LIMIT),
        name="post_sattn" if fused else "post",
    )(*operands)


def _block_diag_heads(q, width):
    nb, ts, _ = q.shape
    q5 = q.reshape(nb, ts, 1, N_HEADS, width)
    keep = jnp.eye(N_HEADS, dtype=bool).reshape(1, 1, N_HEADS, N_HEADS, 1)
    return jnp.where(keep, q5, jnp.zeros((), q.dtype)).reshape(nb, ts * N_HEADS, N_HEADS * width)


def kernel(x_prompt, x_sample, cache_k, cache_v, cache_logf, state_ssm_re, state_ssm_im, page_table,
           c_prompt, c_sample, w_ada, b_ada, norm_mix_g, norm_ffn_g, w_in, b_fgate, q_norm_g, k_norm_g,
           ssm_lambda_re, ssm_lambda_im, ssm_log_dt, ssm_b_re, ssm_b_im, ssm_c_re, ssm_c_im, ssm_d,
           w_glu, b_glu, w_up_a, w_up_s, w_out, w_ffn_up, w_ffn_down):
    nbp, ntp, _ = x_prompt.shape
    nbs, nts, _ = x_sample.shape
    n_pool = cache_k.shape[0]

    a3 = 3 * A_WIDTH
    wf_pad = jnp.pad(w_in[:, a3:a3 + N_HEADS], ((0, 0), (0, LANES - N_HEADS)))
    w2 = jnp.concatenate([w_in[:, :a3], w_in[:, a3 + N_HEADS:], wf_pad], axis=1).astype(BF16)
    bfp = jnp.pad(b_fgate, (0, LANES - N_HEADS)).reshape(1, LANES)
    wt = jnp.concatenate([w_in[:, :A_WIDTH].T, w_in[:, 2 * A_WIDTH:a3].T], axis=0).astype(BF16)
    gqp = jnp.pad(q_norm_g, (0, LANES - D_HEAD)).reshape(1, LANES)
    gq_col = q_norm_g.reshape(D_HEAD, 1)
    gkp = jnp.pad(k_norm_g, (0, LANES - D_HEAD)).reshape(1, LANES)
    heads = jnp.arange(N_HEADS)
    e2k = jnp.zeros((LANES, QA_WIDTH), F32)
    cq = jnp.zeros((1, QA_WIDTH), F32)
    for piece in range(N_SPLIT):
        e2k = e2k.at[piece * N_HEADS + heads, HEAD_PAD * heads + D_HEAD + piece].set(-1.0)
        cq = cq.at[0, HEAD_PAD * heads + D_HEAD + piece].set(1.0)
    e2k = e2k.astype(BF16)
    gmix = norm_mix_g.reshape(1, D_MODEL)
    gffn = norm_ffn_g.reshape(1, D_MODEL)
    post_w = (w_glu.astype(BF16), b_glu.reshape(1, S_WIDTH), w_up_a.astype(BF16), w_up_s.astype(BF16),
              w_out.astype(BF16), w_ffn_up.astype(BF16), w_ffn_down.astype(BF16))

    mod = _adaln(jnp.concatenate([c_prompt, c_sample], axis=0), w_ada.astype(BF16), b_ada)
    mod = mod.reshape(nbp + nbs, 1, 6, D_MODEL)
    mods_p = [mod[:nbp, :, i, :] for i in range(6)]
    mods_s = [mod[nbp:, :, i, :] for i in range(6)]

    ab_re, ab_im, bb_re, bb_im = _s5_disc(ssm_lambda_re, ssm_lambda_im, ssm_log_dt, ssm_b_re, ssm_b_im)
    a_row = jnp.concatenate([ab_re.reshape(1, N_STATE), ab_im.reshape(1, N_STATE)], axis=1)
    eye_g = jnp.eye(N_GROUPS, dtype=F32)

    def b_blockdiag(bb):
        return jnp.einsum('gcp,gh->gchp', bb, eye_g).reshape(S_WIDTH, N_STATE)

    def c_blockdiag(cc):
        return jnp.einsum('gcp,gh->gphc', cc, eye_g).reshape(N_STATE, S_WIDTH)

    bmat = jnp.concatenate([b_blockdiag(bb_re), b_blockdiag(bb_im)], axis=1).astype(BF16)
    cmat = jnp.concatenate([c_blockdiag(ssm_c_re), -c_blockdiag(ssm_c_im)], axis=0).astype(BF16)
    d_row = ssm_d.reshape(1, S_WIDTH)

    tt = min(512, ntp)
    ablk = min(512, ntp)
    bbs = min(64, nbs)
    qa_s, ka_s, k_s, v_s, vb_s, lf_s, u_s, sga_s, sgs_s, qn_s = _inproj(
        x_sample, mods_s[0], mods_s[1], gmix, w2, wt, bfp, gqp, gq_col, gkp, e2k, cq,
        bb=bbs, tt=nts, prompt=False, kblk=ablk)
    lf_rows = jnp.swapaxes(cache_logf, 1, 2).reshape(n_pool * N_HEADS, PAGE)
    rsuf = _pool_logf(lf_rows).reshape(n_pool, N_HEADS, 2 * PAGE)
    cache_kt = jnp.transpose(cache_k, (0, 2, 3, 1)).reshape(n_pool, A_WIDTH, PAGE)
    cache_vt = jnp.transpose(cache_v, (0, 2, 3, 1)).reshape(n_pool, A_WIDTH, PAGE)
    sattn_args = (page_table, _block_diag_heads(qn_s, D_HEAD), _block_diag_heads(qa_s, HEAD_PAD),
                  ka_s, vb_s, cache_kt, cache_vt, rsuf)

    qat, ka, k_p, vt_p, vtb, lf_p, u_tm, sga, sgs = _inproj(
        x_prompt, mods_p[0], mods_p[1], gmix, w2, wt, bfp, gqp, gq_col, gkp, e2k, cq,
        bb=1, tt=tt, prompt=True, kblk=ablk)
    attn_p = _attn(qat, ka, vtb, qblk=min(512, ntp), kblk=ablk)
    v_p = jnp.transpose(vt_p.reshape(nbp, N_HEADS, D_HEAD, ntp), (0, 3, 1, 2))
    h0 = jnp.zeros((nbp, 2 * N_STATE), F32)
    ys_tm, ht_p = _s5_pipelined(u_tm.reshape(ntp, nbp, S_WIDTH), h0, a_row, bmat, cmat, d_row, tc=min(64, ntp))
    y_prompt, attn_s = _post(x_prompt, attn_p, ys_tm.reshape(ntp, nbp * S_WIDTH), sga, sgs,
                             mods_p[2], mods_p[3], mods_p[4], mods_p[5], gffn, *post_w,
                             bb=1, tt=tt, ys_time_major=True, sattn_args=sattn_args)

    h0_s = jnp.concatenate([state_ssm_re.reshape(nbs, N_STATE), state_ssm_im.reshape(nbs, N_STATE)], axis=1)
    ys_s_tm, ht_s = _s5(jnp.swapaxes(u_s, 0, 1), h0_s, a_row, bmat, cmat, d_row, tc=nts)
    y_sample = _post(x_sample, attn_s, jnp.swapaxes(ys_s_tm, 0, 1), sga_s, sgs_s,
                     mods_s[2], mods_s[3], mods_s[4], mods_s[5], gffn, *post_w,
                     bb=bbs, tt=nts, ys_time_major=False)

    def heads4(z):
        return z.reshape(z.shape[0], z.shape[1], N_HEADS, D_HEAD)

    def state3(z):
        return z.reshape(z.shape[0], N_GROUPS, STATE_P)

    return (y_prompt, y_sample, heads4(k_p), heads4(v_p), lf_p,
            state3(ht_p[:, :N_STATE]), state3(ht_p[:, N_STATE:]),
            heads4(k_s), heads4(v_s), lf_s,
            state3(ht_s[:, :N_STATE]), state3(ht_s[:, N_STATE:]))
```
